```python
import functools
import jax, jax.numpy as jnp
from jax import lax
import numpy as np

D_MODEL = 2048
BATCH = 4
SEQ = 2048
DEPTH = 2
DEC_BATCH = 128
DEC_SEQ = 8
PAST_LEN = 16384
PAGE_SIZE = 128

MLA_HEADS = 8
MLA_NOPE = 128
MLA_ROPE = 64
MLA_QK = MLA_NOPE + MLA_ROPE
MLA_V = 128
Q_LORA = 512
KV_LORA = 128
Q_BLOCK = 128
ATTN_SCALE = MLA_QK ** -0.5
RET_HEADS = 4
RET_DK = 128
RET_DV = 256
RET_CHUNK = 128
CONV_D = 1024
CONV_W = 3
A_WIDTH = MLA_HEADS * MLA_V
B_WIDTH = RET_HEADS * RET_DV
C_WIDTH = CONV_D
IN_SPLITS = [Q_LORA, KV_LORA, MLA_ROPE,
             RET_HEADS * RET_DK, RET_HEADS * RET_DK, RET_HEADS * RET_DV, B_WIDTH,
             CONV_D, CONV_D, CONV_D,
             D_MODEL, D_MODEL, D_MODEL]
N_IN = sum(IN_SPLITS)
N_GROUPS = 8
EXPERTS_PER_GROUP = 8
N_EXPERTS = N_GROUPS * EXPERTS_PER_GROUP
TOP_K = 2
EXPERT_FF = 512
MOE_BLOCK = 128
ROPE_BASE = 10000.0
NORM_EPS = 1e-6

kernel_name = 'hybrid_mla_retention_shortconv_hiermoe_step'


def rms_norm(x, g):
    xf = x.astype(jnp.float32)
    y = xf * lax.rsqrt(jnp.mean(xf * xf, axis=-1, keepdims=True) + NORM_EPS)
    return (y * g.astype(jnp.float32)).astype(x.dtype)


def rope(x, pos):
    half = x.shape[-1] // 2
    inv_freq = ROPE_BASE ** (-jnp.arange(half, dtype=jnp.float32) / half)
    ang = pos.astype(jnp.float32)[:, None] * inv_freq[None, :]
    cos = jnp.cos(ang)[:, None, :]
    sin = jnp.sin(ang)[:, None, :]
    xf = x.astype(jnp.float32)
    x1, x2 = xf[..., :half], xf[..., half:]
    return jnp.concatenate([x1 * cos - x2 * sin, x1 * sin + x2 * cos], axis=-1).astype(x.dtype)


def split_in(z):
    return jnp.split(z, np.cumsum(IN_SPLITS)[:-1].tolist(), axis=-1)


def mla_keys(c, kr, w_uk, g_kn):
    kn = (c @ w_uk).reshape(c.shape[:-1] + (MLA_HEADS, MLA_NOPE))
    kr_b = jnp.broadcast_to(kr[..., None, :], kn.shape[:-1] + (MLA_ROPE,)).astype(kn.dtype)
    return rms_norm(jnp.concatenate([kn, kr_b], axis=-1), g_kn)


def mla_project(cq, ckv, kpe, pos, lp):
    b, s = cq.shape[:2]
    q = (rms_norm(cq, lp['mla_g_cq']) @ lp['mla_w_uq']).reshape(b, s, MLA_HEADS, MLA_QK)
    q = jnp.concatenate([q[..., :MLA_NOPE], rope(q[..., MLA_NOPE:], pos)], axis=-1)
    q = rms_norm(q, lp['mla_g_qn'])
    c_lat = rms_norm(ckv, lp['mla_g_ckv'])
    k_rot = rope(kpe[:, :, None, :], pos)[:, :, 0, :]
    k = mla_keys(c_lat, k_rot, lp['mla_w_uk'], lp['mla_g_kn'])
    v = (c_lat @ lp['mla_w_uv']).reshape(b, s, MLA_HEADS, MLA_V)
    return q, k, v, c_lat, k_rot


def mla_attend_causal(q, k, v):
    b, s = q.shape[:2]
    nb = s // Q_BLOCK
    q_blocks = q.reshape(b, nb, Q_BLOCK, MLA_HEADS, MLA_QK).swapaxes(0, 1)
    k_pos = jnp.arange(s)

    def block(args):
        qi, i = args
        sc = jnp.einsum('bqhd,bkhd->bhqk', qi, k).astype(jnp.float32) * ATTN_SCALE
        q_pos = i * Q_BLOCK + jnp.arange(Q_BLOCK)
        sc = jnp.where(k_pos[None, :] <= q_pos[:, None], sc, -jnp.inf)
        p = jax.nn.softmax(sc, axis=-1).astype(v.dtype)
        return jnp.einsum('bhqk,bkhe->bqhe', p, v)

    o = lax.map(block, (q_blocks, jnp.arange(nb)))
    return o.swapaxes(0, 1).reshape(b, s, A_WIDTH)


def mla_attend_paged(q, k, v, *, page_table, cache_ckv, cache_kpe, layer, w_uk, w_uv, g_kn):
    w_uv_h = w_uv.reshape(KV_LORA, MLA_HEADS, MLA_V)

    def one(args):
        qb, kb, vb, pages = args
        c_past = cache_ckv[layer, pages].reshape(-1, KV_LORA)
        kr_past = cache_kpe[layer, pages].reshape(-1, MLA_ROPE)
        k_past = mla_keys(c_past, kr_past, w_uk, g_kn)
        nq = qb.shape[0]
        n_past = c_past.shape[0]
        s_past = jnp.einsum('qhd,khd->hqk', qb, k_past).astype(jnp.float32)
        s_new = jnp.einsum('qhd,khd->hqk', qb, kb).astype(jnp.float32)
        s_new = jnp.where(jnp.tril(jnp.ones((nq, nq), bool)), s_new, -jnp.inf)
        p = jax.nn.softmax(jnp.concatenate([s_past, s_new], axis=-1) * ATTN_SCALE, axis=-1)
        p_past = p[..., :n_past].astype(c_past.dtype)
        p_new = p[..., n_past:].astype(vb.dtype)
        o_lat = jnp.einsum('hqk,kc->qhc', p_past, c_past)
        o = jnp.einsum('qhc,che->qhe', o_lat, w_uv_h) + jnp.einsum('hqk,khe->qhe', p_new, vb)
        return o.reshape(nq, A_WIDTH)

    return lax.map(one, (q, k, v, page_table))


def retention_chunk(q, k, v, R):
    L = q.shape[1]
    log_g = jnp.log1p(-jnp.exp2(-5.0 - jnp.arange(RET_HEADS, dtype=jnp.float32)))
    t = jnp.arange(L, dtype=jnp.float32)
    rel = t[:, None] - t[None, :]
    decay = jnp.where(rel >= 0, jnp.exp(log_g[:, None, None] * jnp.maximum(rel, 0.0)), 0.0)
    s = jnp.einsum('bihd,bjhd->bhij', q, k) * decay
    o = jnp.einsum('bhij,bjhe->bihe', s, v)
    o = o + jnp.einsum('bihd,bhde->bihe', q, R) * jnp.exp(jnp.outer(t + 1.0, log_g))[None, :, :, None]
    k_dec = k * jnp.exp(jnp.outer(L - 1.0 - t, log_g))[None, :, :, None]
    R = jnp.exp(log_g * L)[None, :, None, None] * R + jnp.einsum('bjhd,bjhe->bhde', k_dec, v)
    return o, R


def retention(q, k, v, R0):
    b, L = q.shape[:2]
    chunk = RET_CHUNK if L % RET_CHUNK == 0 else L
    nc = L // chunk

    def to_chunks(a):
        return a.astype(jnp.float32).reshape((b, nc, chunk) + a.shape[2:]).swapaxes(0, 1)

    def step(R, qkv):
        o, R = retention_chunk(qkv[0], qkv[1], qkv[2], R)
        return R, o

    R, o = lax.scan(step, R0.astype(jnp.float32), (to_chunks(q), to_chunks(k), to_chunks(v)))
    return o.swapaxes(0, 1).reshape(b, L, RET_HEADS, RET_DV), R


def short_conv(gate_b, gate_c, u, buf, conv_w):
    w = gate_c * u
    xp = jnp.concatenate([buf.astype(w.dtype), w], axis=1)
    L = w.shape[1]
    y = sum(conv_w[j] * xp[:, j:j + L] for j in range(CONV_W))
    return gate_b * y, xp[:, L:]


def hier_moe(h, lp):
    shp = h.shape
    hf = h.reshape(-1, D_MODEL)
    T = hf.shape[0]
    g_logits = (hf @ lp['router_group_w']).astype(jnp.float32) + lp['router_group_b'].astype(jnp.float32)
    p_group = jax.nn.softmax(g_logits, axis=-1)
    grp = jnp.argmax(g_logits, axis=-1).astype(jnp.int32)
    e_logits = ((hf @ lp['router_expert_w']).astype(jnp.float32)
                + lp['router_expert_b'].astype(jnp.float32)).reshape(T, N_GROUPS, EXPERTS_PER_GROUP)
    in_grp = jnp.take_along_axis(e_logits, grp[:, None, None], axis=1)[:, 0]
    top_v, top_i = lax.top_k(in_grp, TOP_K)
    gate = jax.nn.softmax(top_v, axis=-1) * jnp.take_along_axis(p_group, grp[:, None], axis=1)
    expert = grp[:, None] * EXPERTS_PER_GROUP + top_i.astype(jnp.int32)
    n_assign = T * TOP_K
    e_flat = expert.reshape(-1)
    tok = jnp.repeat(jnp.arange(T, dtype=jnp.int32), TOP_K)
    w_flat = gate.reshape(-1)
    order = jnp.argsort(e_flat)
    e_s, tok_s, w_s = e_flat[order], tok[order], w_flat[order]
    counts = jax.ops.segment_sum(jnp.ones_like(e_flat), e_flat, num_segments=N_EXPERTS)
    start = jnp.cumsum(counts) - counts
    padded = (counts + MOE_BLOCK - 1) // MOE_BLOCK * MOE_BLOCK
    pad_end = jnp.cumsum(padded)
    pad_start = pad_end - padded
    dest = pad_start[e_s] + jnp.arange(n_assign, dtype=jnp.int32) - start[e_s]
    n_blocks = (n_assign + N_EXPERTS * (MOE_BLOCK - 1) + MOE_BLOCK - 1) // MOE_BLOCK
    n_slots = n_blocks * MOE_BLOCK
    slot_tok = jnp.zeros((n_slots,), jnp.int32).at[dest].set(tok_s)
    slot_w = jnp.zeros((n_slots,), jnp.float32).at[dest].set(w_s)
    block_expert = jnp.minimum(
        jnp.searchsorted(pad_end, jnp.arange(n_blocks, dtype=jnp.int32) * MOE_BLOCK, side='right'),
        N_EXPERTS - 1)
    x_blocks = hf[slot_tok].reshape(n_blocks, MOE_BLOCK, D_MODEL)
    w_gate, w_up, w_down = lp['expert_w_gate'], lp['expert_w_up'], lp['expert_w_down']

    def expert_block(args):
        xb, e = args
        return (jax.nn.silu(xb @ w_gate[e]) * (xb @ w_up[e])) @ w_down[e]

    y_blocks = lax.map(expert_block, (x_blocks, block_expert)).reshape(n_slots, D_MODEL)
    out = jnp.zeros_like(hf).at[slot_tok].add(y_blocks * slot_w[:, None].astype(y_blocks.dtype))
    return out.reshape(shp)


def layer_step(x, pos, ret_state, conv_buf, attend, lp):
    b, s = x.shape[:2]
    (cq, ckv, kpe, rq, rk, rv, rg, cb, cc, cu, ga, gb, gc) = split_in(rms_norm(x, lp['norm_mix_g']) @ lp['w_in'])
    q, k, v, c_lat, k_rot = mla_project(cq, ckv, kpe, pos, lp)
    o_a = attend(q, k, v)
    rq = rope(rq.reshape(b, s, RET_HEADS, RET_DK), pos)
    rk = rope(rk.reshape(b, s, RET_HEADS, RET_DK), pos) * (RET_DK ** -0.5)
    o_r, ret_new = retention(rq, rk, rv.reshape(b, s, RET_HEADS, RET_DV), ret_state)
    o_b = jax.nn.silu(rg) * rms_norm(o_r, lp['ret_g']).reshape(b, s, B_WIDTH).astype(x.dtype)
    o_c, conv_new = short_conv(cb, cc, cu, conv_buf, lp['conv_w'])
    m = (jax.nn.sigmoid(ga) * (o_a @ lp['w_branch_a'])
         + jax.nn.sigmoid(gb) * (o_b @ lp['w_branch_b'])
         + jax.nn.sigmoid(gc) * (o_c @ lp['w_branch_c']))
    x = x + m @ lp['w_out']
    x = x + hier_moe(rms_norm(x, lp['norm_ffn_g']), lp)
    return x, c_lat, k_rot, ret_new.astype(x.dtype), conv_new


def setup_inputs(seed: int = 0) -> dict:
    key = jax.random.key(seed)
    keys = jax.random.split(key, 32)
    counter = iter(range(32))

    def nrm(shape, scale):
        return jax.random.normal(keys[next(counter)], shape, jnp.float32) * scale

    def gain(shape):
        return 1.0 + 0.05 * jax.random.normal(keys[next(counter)], shape, jnp.float32)

    n_pages = PAST_LEN // PAGE_SIZE
    n_used = DEC_BATCH * n_pages
    n_pool = n_used + n_used // 4
    x_prompt = nrm((BATCH, SEQ, D_MODEL), 1.0)
    x_sample = nrm((DEC_BATCH, DEC_SEQ, D_MODEL), 1.0)
    cache_ckv = nrm((DEPTH, n_pool, PAGE_SIZE, KV_LORA), 1.0)
    cache_kpe = nrm((DEPTH, n_pool, PAGE_SIZE, MLA_ROPE), 1.0)
    state_ret = nrm((DEPTH, DEC_BATCH, RET_HEADS, RET_DK, RET_DV), 0.5)
    state_conv = nrm((DEPTH, DEC_BATCH, CONV_W - 1, CONV_D), 0.5)
    page_table = jax.random.permutation(keys[next(counter)], n_pool)[:n_used].reshape(DEC_BATCH, n_pages).astype(jnp.int32)
    norm_mix_g = gain((DEPTH, D_MODEL))
    w_in = nrm((DEPTH, D_MODEL, N_IN), D_MODEL ** -0.5)
    mla_g_cq = gain((DEPTH, Q_LORA))
    mla_g_ckv = gain((DEPTH, KV_LORA))
    mla_w_uq = nrm((DEPTH, Q_LORA, MLA_HEADS * MLA_QK), Q_LORA ** -0.5)
    mla_w_uk = nrm((DEPTH, KV_LORA, MLA_HEADS * MLA_NOPE), KV_LORA ** -0.5)
    mla_w_uv = nrm((DEPTH, KV_LORA, MLA_HEADS * MLA_V), KV_LORA ** -0.5)
    mla_g_qn = gain((DEPTH, MLA_QK))
    mla_g_kn = gain((DEPTH, MLA_QK))
    ret_g = gain((DEPTH, RET_HEADS, RET_DV))
    conv_w = nrm((DEPTH, CONV_W, CONV_D), 0.5)
    w_branch_a = nrm((DEPTH, A_WIDTH, D_MODEL), A_WIDTH ** -0.5)
    w_branch_b = nrm((DEPTH, B_WIDTH, D_MODEL), B_WIDTH ** -0.5)
    w_branch_c = nrm((DEPTH, C_WIDTH, D_MODEL), C_WIDTH ** -0.5)
    w_out = nrm((DEPTH, D_MODEL, D_MODEL), D_MODEL ** -0.5)
    norm_ffn_g = gain((DEPTH, D_MODEL))
    router_group_w = nrm((DEPTH, D_MODEL, N_GROUPS), D_MODEL ** -0.5)
    router_group_b = nrm((DEPTH, N_GROUPS), 0.01)
    router_expert_w = nrm((DEPTH, D_MODEL, N_EXPERTS), D_MODEL ** -0.5)
    router_expert_b = nrm((DEPTH, N_EXPERTS), 0.01)
    expert_w_gate = nrm((DEPTH, N_EXPERTS, D_MODEL, EXPERT_FF), D_MODEL ** -0.5)
    expert_w_up = nrm((DEPTH, N_EXPERTS, D_MODEL, EXPERT_FF), D_MODEL ** -0.5)
    expert_w_down = nrm((DEPTH, N_EXPERTS, EXPERT_FF, D_MODEL), EXPERT_FF ** -0.5)
    return {'x_prompt': x_prompt, 'x_sample': x_sample, 'cache_ckv': cache_ckv, 'cache_kpe': cache_kpe,
            'state_ret': state_ret, 'state_conv': state_conv, 'page_table': page_table,
            'norm_mix_g': norm_mix_g, 'w_in': w_in, 'mla_g_cq': mla_g_cq, 'mla_g_ckv': mla_g_ckv,
            'mla_w_uq': mla_w_uq, 'mla_w_uk': mla_w_uk, 'mla_w_uv': mla_w_uv, 'mla_g_qn': mla_g_qn,
            'mla_g_kn': mla_g_kn, 'ret_g': ret_g, 'conv_w': conv_w, 'w_branch_a': w_branch_a,
            'w_branch_b': w_branch_b, 'w_branch_c': w_branch_c, 'w_out': w_out, 'norm_ffn_g': norm_ffn_g,
            'router_group_w': router_group_w, 'router_group_b': router_group_b,
            'router_expert_w': router_expert_w, 'router_expert_b': router_expert_b,
            'expert_w_gate': expert_w_gate, 'expert_w_up': expert_w_up, 'expert_w_down': expert_w_down}


def reference(x_prompt, x_sample, cache_ckv, cache_kpe, state_ret, state_conv, page_table,
              norm_mix_g, w_in, mla_g_cq, mla_g_ckv, mla_w_uq, mla_w_uk, mla_w_uv, mla_g_qn, mla_g_kn,
              ret_g, conv_w, w_branch_a, w_branch_b, w_branch_c, w_out, norm_ffn_g,
              router_group_w, router_group_b, router_expert_w, router_expert_b,
              expert_w_gate, expert_w_up, expert_w_down):
    past_len = page_table.shape[1] * PAGE_SIZE
    pos_p = jnp.arange(x_prompt.shape[1], dtype=jnp.int32)
    pos_s = past_len + jnp.arange(x_sample.shape[1], dtype=jnp.int32)
    bp = x_prompt.shape[0]
    xp, xs = x_prompt, x_sample
    ckv_p, kpe_p, ret_p, conv_p = [], [], [], []
    ckv_s, kpe_s, ret_s, conv_s = [], [], [], []
    for l in range(DEPTH):
        lp = {'norm_mix_g': norm_mix_g[l], 'w_in': w_in[l], 'mla_g_cq': mla_g_cq[l], 'mla_g_ckv': mla_g_ckv[l],
              'mla_w_uq': mla_w_uq[l], 'mla_w_uk': mla_w_uk[l], 'mla_w_uv': mla_w_uv[l],
              'mla_g_qn': mla_g_qn[l], 'mla_g_kn': mla_g_kn[l], 'ret_g': ret_g[l], 'conv_w': conv_w[l],
              'w_branch_a': w_branch_a[l], 'w_branch_b': w_branch_b[l], 'w_branch_c': w_branch_c[l],
              'w_out': w_out[l], 'norm_ffn_g': norm_ffn_g[l],
              'router_group_w': router_group_w[l], 'router_group_b': router_group_b[l],
              'router_expert_w': router_expert_w[l], 'router_expert_b': router_expert_b[l],
              'expert_w_gate': expert_w_gate[l], 'expert_w_up': expert_w_up[l], 'expert_w_down': expert_w_down[l]}
        ret0 = jnp.zeros((bp, RET_HEADS, RET_DK, RET_DV), jnp.float32)
        buf0 = jnp.zeros((bp, CONV_W - 1, CONV_D), xp.dtype)
        xp, c1, k1, r1, b1 = layer_step(xp, pos_p, ret0, buf0, mla_attend_causal, lp)
        attend_s = functools.partial(mla_attend_paged, page_table=page_table, cache_ckv=cache_ckv,
                                     cache_kpe=cache_kpe, layer=l, w_uk=lp['mla_w_uk'],
                                     w_uv=lp['mla_w_uv'], g_kn=lp['mla_g_kn'])
        xs, c2, k2, r2, b2 = layer_step(xs, pos_s, state_ret[l], state_conv[l], attend_s, lp)
        ckv_p.append(c1); kpe_p.append(k1); ret_p.append(r1); conv_p.append(b1)
        ckv_s.append(c2); kpe_s.append(k2); ret_s.append(r2); conv_s.append(b2)
    return (xp, xs,
            jnp.stack(ckv_p), jnp.stack(kpe_p), jnp.stack(ret_p), jnp.stack(conv_p),
            jnp.stack(ckv_s), jnp.stack(kpe_s), jnp.stack(ret_s), jnp.stack(conv_s))
```

```python
import functools
import math

import numpy as np
import jax
import jax.numpy as jnp
from jax import lax
from jax.experimental import pallas as pl
from jax.experimental.pallas import tpu as pltpu

BF = jnp.bfloat16
F32 = jnp.float32

MLA_HEADS = 8
MLA_NOPE = 128
MLA_ROPE = 64
MLA_QK = MLA_NOPE + MLA_ROPE
MLA_V = 128
Q_LORA = 512
KV_LORA = 128
RET_HEADS = 4
RET_DK = 128
RET_DV = 256
RET_CHUNK = 128
CONV_D = 1024
CONV_W = 3
N_GROUPS = 8
EXPERTS_PER_GROUP = 8
N_EXPERTS = N_GROUPS * EXPERTS_PER_GROUP
TOP_K = 2
EXPERT_FF = 512
PAGE_SIZE = 128
ROPE_BASE = 10000.0
NORM_EPS = 1e-6
ATTN_SCALE = MLA_QK ** -0.5
LOG_G = tuple(math.log1p(-(2.0 ** (-5.0 - h))) for h in range(RET_HEADS))

LANES = 128
QPAD = 256
VMEM_LIMIT = 56 * 1024 * 1024
MOE_BM = 128
NEG_INF = float("-inf")

_NT = (((1,), (1,)), ((), ()))


def _cparams(sem, vmem=VMEM_LIMIT):
    return pltpu.CompilerParams(dimension_semantics=sem, vmem_limit_bytes=vmem)


def _pick(n, target, mult=8):
    if n <= target:
        return n
    for d in range(target, 0, -1):
        if n % d == 0 and d % mult == 0:
            return d
    return n


def _rms(x, g):
    return x * lax.rsqrt(jnp.mean(x * x, axis=-1, keepdims=True) + NORM_EPS) * g


def _lane_sumsq_t(x):
    sq = x * x
    hi = sq.astype(BF)
    r1 = sq - hi.astype(F32)
    mid = r1.astype(BF)
    lo = (r1 - mid.astype(F32)).astype(BF)
    ones = jnp.ones((8, x.shape[1]), BF)
    nt = lambda b: lax.dot_general(ones, b, _NT, preferred_element_type=F32)
    return nt(hi) + (nt(mid) + nt(lo))


def _norm_kernel(x_ref, g_ref, o_ref):
    o_ref[...] = _rms(x_ref[...], g_ref[...]).astype(o_ref.dtype)


def _norm_cast(x, g):
    t, d = x.shape
    tm = _pick(t, 512)
    return pl.pallas_call(
        _norm_kernel,
        out_shape=jax.ShapeDtypeStruct((t, d), BF),
        grid=(t // tm,),
        in_specs=[pl.BlockSpec((tm, d), lambda i: (i, 0)), pl.BlockSpec((1, d), lambda i: (0, 0))],
        out_specs=pl.BlockSpec((tm, d), lambda i: (i, 0)),
        compiler_params=_cparams(("parallel",)),
        name="norm_cast",
    )(x, g.reshape(1, d))


def _mm_kernel(a_ref, w_ref, o_ref):
    o_ref[...] = jnp.dot(a_ref[...].astype(BF), w_ref[...].astype(BF),
                         preferred_element_type=F32).astype(o_ref.dtype)


def _mm_res_kernel(a_ref, w_ref, r_ref, o_ref):
    o_ref[...] = r_ref[...] + jnp.dot(a_ref[...].astype(BF), w_ref[...].astype(BF),
                                      preferred_element_type=F32)


def _matmul(a, w, residual=None, tm_target=2304, tn_target=256, name="matmul"):
    m, k = a.shape
    n = w.shape[1]
    tm = _pick(m, tm_target, 16)
    tn = _pick(n, tn_target, LANES)
    in_specs = [pl.BlockSpec((tm, k), lambda i, j: (i, 0)), pl.BlockSpec((k, tn), lambda i, j: (0, j))]
    args = [a, w]
    body = _mm_kernel
    if residual is not None:
        in_specs.append(pl.BlockSpec((tm, tn), lambda i, j: (i, j)))
        args.append(residual)
        body = _mm_res_kernel
    return pl.pallas_call(
        body,
        out_shape=jax.ShapeDtypeStruct((m, n), F32),
        grid=(m // tm, n // tn),
        in_specs=in_specs,
        out_specs=pl.BlockSpec((tm, tn), lambda i, j: (i, j)),
        compiler_params=_cparams(("parallel", "arbitrary")),
        name=name,
    )(*args)


def _prep_kernel(cq_ref, ckv_ref, kpe_ref, c_ref, s1_ref, s2_ref, gcq_ref, gckv_ref, wuq_ref, wukt_ref,
                 gqn_ref, gqr_ref, gkn_ref, gkr_ref,
                 clat_ref, krot_ref, q_ref, kcat_ref, rs_ref):
    tm = cq_ref.shape[0]
    cs, s1, s2 = c_ref[...], s1_ref[...], s2_ref[...]

    def rope64(x):
        return x * cs + pltpu.roll(x, 96, 1) * s1 + pltpu.roll(x, 32, 1) * s2

    low = lax.broadcasted_iota(jnp.int32, (tm, LANES), 1) < MLA_ROPE

    c_lat = _rms(ckv_ref[...], gckv_ref[...])
    clat_ref[...] = c_lat
    k_rot = jnp.where(low, rope64(kpe_ref[...]), 0.0)
    krot_ref[...] = k_rot[:, :MLA_ROPE]
    clb = c_lat.astype(BF)
    kcat_ref[:, 0:KV_LORA] = clb
    kcat_ref[:, KV_LORA:QPAD] = k_rot.astype(BF)

    knt = lax.dot_general(wukt_ref[...], clb, _NT, preferred_element_type=F32)
    ssq = jnp.sum((knt * knt).reshape(MLA_HEADS, MLA_NOPE, tm), axis=1)
    rs_ref[...] = lax.rsqrt((ssq + _lane_sumsq_t(k_rot)) / MLA_QK + NORM_EPS)

    cqn = _rms(cq_ref[...], gcq_ref[...]).astype(BF)
    q = jnp.dot(cqn, wuq_ref[...], preferred_element_type=F32)
    gqn, gqr, gkn, gkr = gqn_ref[...], gqr_ref[...], gkn_ref[...], gkr_ref[...]
    nope_w = MLA_HEADS * MLA_NOPE
    for j in range(MLA_HEADS // 2):
        xr = rope64(q[:, nope_w + LANES * j: nope_w + LANES * (j + 1)])
        sq = xr * xr
        ss = (jnp.sum(jnp.where(low, sq, 0.0), axis=-1, keepdims=True),
              jnp.sum(jnp.where(low, 0.0, sq), axis=-1, keepdims=True))
        for e in range(2):
            h = 2 * j + e
            nope = q[:, MLA_NOPE * h: MLA_NOPE * (h + 1)]
            r = lax.rsqrt((jnp.sum(nope * nope, axis=-1, keepdims=True) + ss[e]) / MLA_QK + NORM_EPS)
            qg = (nope * r * gqn * gkn).astype(BF)
            q_ref[h, :, 0:KV_LORA] = jnp.dot(qg, wukt_ref[MLA_NOPE * h: MLA_NOPE * (h + 1), :],
                                             preferred_element_type=F32)
            rp = xr * r * gqr * gkr
            if e == 1:
                rp = pltpu.roll(rp, MLA_ROPE, 1)
            q_ref[h, :, KV_LORA:QPAD] = jnp.where(low, rp, 0.0)


def _mla_prep(z, offs, tabs, gcq, gckv, wuq_p, wukt, gqn, gqr, gkn, gkr):
    t = z.shape[0]
    tm = _pick(t, 256, 16)
    cs, s1, s2 = tabs

    def zs(width, off):
        return pl.BlockSpec((tm, width), lambda i: (i, off // width))

    row = lambda w: pl.BlockSpec((tm, w), lambda i: (i, 0))
    full = lambda a: pl.BlockSpec(a.shape, lambda i: (0,) * a.ndim)
    small = [gcq, gckv, wuq_p, wukt, gqn, gqr, gkn, gkr]
    return pl.pallas_call(
        _prep_kernel,
        out_shape=(jax.ShapeDtypeStruct((t, KV_LORA), F32),
                   jax.ShapeDtypeStruct((t, MLA_ROPE), F32),
                   jax.ShapeDtypeStruct((MLA_HEADS, t, QPAD), F32),
                   jax.ShapeDtypeStruct((t, QPAD), BF),
                   jax.ShapeDtypeStruct((MLA_HEADS, t), F32)),
        grid=(t // tm,),
        in_specs=[zs(Q_LORA, offs["cq"]), zs(KV_LORA, offs["ckv"]), zs(LANES, offs["kpe"]),
                  row(LANES), row(LANES), row(LANES)] + [full(a) for a in small],
        out_specs=(row(KV_LORA), row(MLA_ROPE),
                   pl.BlockSpec((MLA_HEADS, tm, QPAD), lambda i: (0, i, 0)),
                   row(QPAD),
                   pl.BlockSpec((MLA_HEADS, tm), lambda i: (0, i))),
        compiler_params=_cparams(("parallel",)),
        name="mla_prep",
    )(z, z, z, cs, s1, s2, *small)


def _attn_prompt_kernel(q_ref, k_ref, rs_ref, wuv_ref, o_ref, qb_sc, m_sc, l_sc, acc_sc, *, bq, bk):
    qi = pl.program_id(1)
    ki = pl.program_id(2)
    rows = MLA_HEADS * bq

    @pl.when(ki == 0)
    def _():
        qb_sc[...] = q_ref[...].reshape(rows, QPAD).astype(BF)
        m_sc[...] = jnp.full(m_sc.shape, NEG_INF, F32)
        l_sc[...] = jnp.zeros(l_sc.shape, F32)
        acc_sc[...] = jnp.zeros(acc_sc.shape, F32)

    @pl.when(ki * bk <= qi * bq + (bq - 1))
    def _():
        k = k_ref[...]
        s = lax.dot_general(qb_sc[...], k, _NT, preferred_element_type=F32)
        rs = rs_ref[...] * ATTN_SCALE
        s3 = s.reshape(MLA_HEADS, bq, bk) * rs[:, None, :]
        qpos = qi * bq + lax.broadcasted_iota(jnp.int32, (1, bq, bk), 1)
        kpos = ki * bk + lax.broadcasted_iota(jnp.int32, (1, bq, bk), 2)
        s2 = jnp.where(kpos <= qpos, s3, NEG_INF).reshape(rows, bk)
        m_prev = m_sc[...]
        m_new = jnp.maximum(m_prev, jnp.max(s2, axis=-1, keepdims=True))
        alpha = jnp.exp(m_prev - m_new)
        p = jnp.exp(s2 - m_new)
        l_sc[...] = alpha * l_sc[...] + jnp.sum(p, axis=-1, keepdims=True)
        acc_sc[...] = alpha * acc_sc[...] + jnp.dot(p.astype(BF), k[:, 0:KV_LORA],
                                                    preferred_element_type=F32)
        m_sc[...] = m_new

    @pl.when(ki == pl.num_programs(2) - 1)
    def _():
        ol = (acc_sc[...] / l_sc[...]).astype(BF)
        for h in range(MLA_HEADS):
            o_ref[:, MLA_V * h: MLA_V * (h + 1)] = jnp.dot(
                ol[h * bq:(h + 1) * bq], wuv_ref[:, MLA_V * h: MLA_V * (h + 1)],
                preferred_element_type=F32).astype(o_ref.dtype)


def _attn_prompt(q, kcat, rs, wuv, nb, s):
    bq = _pick(s, 128, 16)
    bk = _pick(s, 512, LANES)
    nq, nk = s // bq, s // bk
    rows = MLA_HEADS * bq

    def kblk(b, qi, ki):
        return b * nk + jnp.minimum(ki, (qi * bq + bq - 1) // bk)

    return pl.pallas_call(
        functools.partial(_attn_prompt_kernel, bq=bq, bk=bk),
        out_shape=jax.ShapeDtypeStruct((nb * s, MLA_HEADS * MLA_V), BF),
        grid=(nb, nq, nk),
        in_specs=[pl.BlockSpec((MLA_HEADS, bq, QPAD), lambda b, qi, ki: (0, b * nq + qi, 0)),
                  pl.BlockSpec((bk, QPAD), lambda b, qi, ki: (kblk(b, qi, ki), 0)),
                  pl.BlockSpec((MLA_HEADS, bk), lambda b, qi, ki: (0, kblk(b, qi, ki))),
                  pl.BlockSpec(wuv.shape, lambda b, qi, ki: (0, 0))],
        out_specs=pl.BlockSpec((bq, MLA_HEADS * MLA_V), lambda b, qi, ki: (b * nq + qi, 0)),
        scratch_shapes=[pltpu.VMEM((rows, QPAD), BF), pltpu.VMEM((rows, 1), F32),
                        pltpu.VMEM((rows, 1), F32), pltpu.VMEM((rows, KV_LORA), F32)],
        compiler_params=_cparams(("parallel", "parallel", "arbitrary")),
        name="attn_prompt",
    )(q, kcat, rs, wuv)


def _attn_paged_kernel(pt_ref, q_ref, cnew_ref, krnew_ref, wukt_ref, wuv_ref, *rest, pp, dq):
    c_refs = rest[:pp]
    kr_refs = rest[pp:2 * pp]
    o_ref = rest[2 * pp]
    qa_sc, qr_sc, m_sc, l_sc, acc_sc, cpad_sc, krpad_sc = rest[2 * pp + 1:]
    j = pl.program_id(1)
    rows = MLA_HEADS * dq

    @pl.when(j == 0)
    def _():
        q = q_ref[...].reshape(rows, QPAD)
        qa_sc[...] = q[:, 0:KV_LORA].astype(BF)
        qr_sc[...] = q[:, KV_LORA:KV_LORA + MLA_ROPE].astype(BF)
        m_sc[...] = jnp.full(m_sc.shape, NEG_INF, F32)
        l_sc[...] = jnp.zeros(l_sc.shape, F32)
        acc_sc[...] = jnp.zeros(acc_sc.shape, F32)

    def update(c, kr, mask):
        n = c.shape[0]
        cb = c.astype(BF)
        knt = lax.dot_general(wukt_ref[...], cb, _NT, preferred_element_type=F32)
        ssq = jnp.sum((knt * knt).reshape(MLA_HEADS, MLA_NOPE, n), axis=1)
        rs = lax.rsqrt((ssq + _lane_sumsq_t(kr)) / MLA_QK + NORM_EPS) * ATTN_SCALE
        s = (lax.dot_general(qa_sc[...], cb, _NT, preferred_element_type=F32)
             + lax.dot_general(qr_sc[...], kr.astype(BF), _NT, preferred_element_type=F32))
        s = (s.reshape(MLA_HEADS, dq, n) * rs[:, None, :]).reshape(rows, n)
        if mask is not None:
            s = jnp.where(mask, s, NEG_INF)
        m_prev = m_sc[...]
        m_new = jnp.maximum(m_prev, jnp.max(s, axis=-1, keepdims=True))
        alpha = jnp.exp(m_prev - m_new)
        p = jnp.exp(s - m_new)
        l_sc[...] = alpha * l_sc[...] + jnp.sum(p, axis=-1, keepdims=True)
        acc_sc[...] = alpha * acc_sc[...] + jnp.dot(p.astype(BF), cb, preferred_element_type=F32)
        m_sc[...] = m_new

    step = 2 if pp % 2 == 0 else 1
    for i in range(0, pp, step):
        c = jnp.concatenate([c_refs[i + u][...] for u in range(step)], axis=0)
        kr = jnp.concatenate([kr_refs[i + u][...] for u in range(step)], axis=0)
        update(c, kr, None)

    @pl.when(j == pl.num_programs(1) - 1)
    def _():
        cpad_sc[...] = jnp.zeros(cpad_sc.shape, F32)
        krpad_sc[...] = jnp.zeros(krpad_sc.shape, F32)
        cpad_sc[0:dq, :] = cnew_ref[...]
        krpad_sc[0:dq, :] = krnew_ref[...]
        t = lax.broadcasted_iota(jnp.int32, (rows, PAGE_SIZE), 1)
        r = lax.broadcasted_iota(jnp.int32, (MLA_HEADS, dq, PAGE_SIZE), 1).reshape(rows, PAGE_SIZE)
        update(cpad_sc[...], krpad_sc[...], t <= r)
        ol = (acc_sc[...] / l_sc[...]).astype(BF)
        full = jnp.dot(ol, wuv_ref[...], preferred_element_type=F32)
        for h in range(MLA_HEADS):
            o_ref[:, MLA_V * h: MLA_V * (h + 1)] = full[h * dq:(h + 1) * dq, MLA_V * h: MLA_V * (h + 1)]


def _attn_paged(page_table, q, c_new, kr_new, wukt, wuv, cache_ckv, cache_kpe, layer, row_blk0):
    nseq, n_pages = page_table.shape
    dq = c_new.shape[1]
    pp = next(p for p in (8, 4, 2, 1) if n_pages % p == 0)
    n_chunks = n_pages // pp
    pt = page_table.reshape(-1)

    def cache_spec(width, i):
        return pl.BlockSpec((None, None, PAGE_SIZE, width),
                            lambda b, j, pt_ref: (layer, pt_ref[b * n_pages + j * pp + i], 0, 0))

    in_specs = [pl.BlockSpec((MLA_HEADS, dq, QPAD), lambda b, j, pt_ref: (0, row_blk0 + b, 0)),
                pl.BlockSpec((None, dq, KV_LORA), lambda b, j, pt_ref: (b, 0, 0)),
                pl.BlockSpec((None, dq, MLA_ROPE), lambda b, j, pt_ref: (b, 0, 0)),
                pl.BlockSpec(wukt.shape, lambda b, j, pt_ref: (0, 0)),
                pl.BlockSpec(wuv.shape, lambda b, j, pt_ref: (0, 0))]
    in_specs += [cache_spec(KV_LORA, i) for i in range(pp)]
    in_specs += [cache_spec(MLA_ROPE, i) for i in range(pp)]
    rows = MLA_HEADS * dq
    return pl.pallas_call(
        functools.partial(_attn_paged_kernel, pp=pp, dq=dq),
        out_shape=jax.ShapeDtypeStruct((nseq, dq, MLA_HEADS * MLA_V), F32),
        grid_spec=pltpu.PrefetchScalarGridSpec(
            num_scalar_prefetch=1,
            grid=(nseq, n_chunks),
            in_specs=in_specs,
            out_specs=pl.BlockSpec((None, dq, MLA_HEADS * MLA_V), lambda b, j, pt_ref: (b, 0, 0)),
            scratch_shapes=[pltpu.VMEM((rows, KV_LORA), BF), pltpu.VMEM((rows, MLA_ROPE), BF),
                            pltpu.VMEM((rows, 1), F32), pltpu.VMEM((rows, 1), F32),
                            pltpu.VMEM((rows, KV_LORA), F32),
                            pltpu.VMEM((PAGE_SIZE, KV_LORA), F32), pltpu.VMEM((PAGE_SIZE, MLA_ROPE), F32)]),
        compiler_params=_cparams(("parallel", "arbitrary")),
        name="attn_paged",
    )(pt, q, c_new, kr_new, wukt, wuv, *([cache_ckv] * pp), *([cache_kpe] * pp))


def _ret_kernel(rq_ref, rk_ref, rv_ref, rg_ref, cf_ref, sf_ref, g_ref, *rest, chunk, sb, has_state):
    if has_state:
        st_ref, o_ref, so_ref, r_sc, kp_sc, kd_sc, vp_sc, o_sc = rest
    else:
        o_ref, so_ref, r_sc, kp_sc, kd_sc, vp_sc, o_sc = rest
    ci = pl.program_id(1)
    lp = kp_sc.shape[0]

    @pl.when(ci == 0)
    def _():
        if has_state:
            r_sc[...] = st_ref[...]
        else:
            r_sc[...] = jnp.zeros(r_sc.shape, F32)

    if chunk < lp:
        kp_sc[...] = jnp.zeros(kp_sc.shape, F32)
        kd_sc[...] = jnp.zeros(kd_sc.shape, F32)
        vp_sc[...] = jnp.zeros(vp_sc.shape, F32)

    def pad(x, sc):
        if chunk == lp:
            return x
        sc[0:chunk, :] = x
        return sc[...]

    t = lax.broadcasted_iota(jnp.int32, (chunk, 1), 0).astype(F32)
    rel = (lax.broadcasted_iota(jnp.int32, (chunk, lp), 0)
           - lax.broadcasted_iota(jnp.int32, (chunk, lp), 1)).astype(F32)
    for si in range(sb):
        rows = slice(si * chunk, (si + 1) * chunk)
        cf, sf = cf_ref[rows, :], sf_ref[rows, :]
        rope = lambda x: x * cf + pltpu.roll(x, RET_DK // 2, 1) * sf
        for h in range(RET_HEADS):
            lg = LOG_G[h]
            q = rope(rq_ref[rows, RET_DK * h: RET_DK * (h + 1)])
            k = rope(rk_ref[rows, RET_DK * h: RET_DK * (h + 1)]) * (RET_DK ** -0.5)
            v = rv_ref[rows, RET_DV * h: RET_DV * (h + 1)]
            qb = q.astype(BF)
            kp = pad(k, kp_sc).astype(BF)
            vp = pad(v, vp_sc).astype(BF)
            decay = jnp.where(rel >= 0.0, jnp.exp(lg * jnp.maximum(rel, 0.0)), 0.0)
            s = lax.dot_general(qb, kp, _NT, preferred_element_type=F32) * decay
            r_old = r_sc[si, h]
            o = jnp.dot(s.astype(BF), vp, preferred_element_type=F32)
            o = o + jnp.dot(qb, r_old.astype(BF), preferred_element_type=F32) * jnp.exp(lg * (t + 1.0))
            kd = pad(k * jnp.exp(lg * (chunk - 1.0 - t)), kd_sc)
            r_sc[si, h] = math.exp(lg * chunk) * r_old + jnp.dot(
                kd.T.astype(BF), vp, preferred_element_type=F32)
            on = _rms(o, g_ref[h:h + 1, :])
            rg = rg_ref[rows, RET_DV * h: RET_DV * (h + 1)]
            o_sc[rows, RET_DV * h: RET_DV * (h + 1)] = rg * jax.nn.sigmoid(rg) * on
    o_ref[...] = o_sc[...].astype(o_ref.dtype)

    @pl.when(ci == pl.num_programs(1) - 1)
    def _():
        so_ref[...] = r_sc[...]


def _retention(z, offs, tabs, ret_g, nseq, seq_len, row0, state):
    chunk = RET_CHUNK if seq_len % RET_CHUNK == 0 else seq_len
    nc = seq_len // chunk
    sb = 1 if nc > 1 else _pick(nseq, max(1, 64 // chunk), 1)
    rows = sb * chunk
    lp = max(chunk, LANES)
    blk0 = row0 // rows
    cf, sf = tabs
    dk, dv = RET_HEADS * RET_DK, RET_HEADS * RET_DV

    def zs(width, off):
        return pl.BlockSpec((rows, width), lambda b, c: (blk0 + b * nc + c, off // width))

    tab = pl.BlockSpec((rows, RET_DK), lambda b, c: (blk0 + b * nc + c, 0))
    st_spec = pl.BlockSpec((sb, RET_HEADS, RET_DK, RET_DV), lambda b, c: (b, 0, 0, 0))
    in_specs = [zs(dk, offs["rq"]), zs(dk, offs["rk"]), zs(dv, offs["rv"]), zs(dv, offs["rg"]), tab, tab,
                pl.BlockSpec(ret_g.shape, lambda b, c: (0, 0))]
    args = [z, z, z, z, cf, sf, ret_g]
    if state is not None:
        in_specs.append(st_spec)
        args.append(state)
    return pl.pallas_call(
        functools.partial(_ret_kernel, chunk=chunk, sb=sb, has_state=state is not None),
        out_shape=(jax.ShapeDtypeStruct((nseq * seq_len, dv), BF),
                   jax.ShapeDtypeStruct((nseq, RET_HEADS, RET_DK, RET_DV), F32)),
        grid=(nseq // sb, nc),
        in_specs=in_specs,
        out_specs=(pl.BlockSpec((rows, dv), lambda b, c: (b * nc + c, 0)), st_spec),
        scratch_shapes=[pltpu.VMEM((sb, RET_HEADS, RET_DK, RET_DV), F32),
                        pltpu.VMEM((lp, RET_DK), F32), pltpu.VMEM((lp, RET_DK), F32),
                        pltpu.VMEM((lp, RET_DV), F32), pltpu.VMEM((rows, dv), F32)],
        compiler_params=_cparams(("parallel", "arbitrary")),
        name="retention",
    )(*args)


def _conv_kernel(cb_ref, cc_ref, cu_ref, w_ref, *rest, sb, rows_per_seq, has_state):
    if has_state:
        buf_ref, o_ref, so_ref, carry_sc = rest
    else:
        o_ref, so_ref, carry_sc = rest
    ti = pl.program_id(1)
    ln = rows_per_seq
    cd = cb_ref.shape[1]

    @pl.when(ti == 0)
    def _():
        if has_state:
            carry_sc[:, 6:8, :] = buf_ref[...]
        else:
            carry_sc[...] = jnp.zeros(carry_sc.shape, F32)

    w2 = cc_ref[...] * cu_ref[...]
    w3 = w2.reshape(sb, ln, cd)
    r1 = pltpu.roll(w2, 1, 0).reshape(sb, ln, cd)
    r2 = pltpu.roll(w2, 2, 0).reshape(sb, ln, cd)
    prev = carry_sc[...]
    p1, p2 = prev[:, 7:8, :], prev[:, 6:7, :]
    tpos = lax.broadcasted_iota(jnp.int32, (sb, ln, cd), 1)
    x1 = jnp.where(tpos == 0, p1, r1)
    x2 = jnp.where(tpos == 0, p2, jnp.where(tpos == 1, p1, r2))
    cw = w_ref[...]
    y = cw[0:1, :] * x2 + cw[1:2, :] * x1 + cw[2:3, :] * w3
    o_ref[...] = (cb_ref[...] * y.reshape(sb * ln, cd)).astype(o_ref.dtype)
    tail = w3[:, ln - 2:ln, :]
    carry_sc[:, 6:8, :] = tail

    @pl.when(ti == pl.num_programs(1) - 1)
    def _():
        so_ref[...] = tail


def _short_conv(z, offs, conv_w, nseq, seq_len, row0, state):
    if seq_len > 512:
        sb, ln = 1, _pick(seq_len, 512)
    else:
        ln = seq_len
        sb = _pick(nseq, max(1, 256 // ln), 1)
    nt = seq_len // ln
    rows = sb * ln
    blk0 = row0 // rows

    def zs(off):
        return pl.BlockSpec((rows, CONV_D), lambda b, t: (blk0 + b * nt + t, off // CONV_D))

    st_spec = pl.BlockSpec((sb, CONV_W - 1, CONV_D), lambda b, t: (b, 0, 0))
    in_specs = [zs(offs["cb"]), zs(offs["cc"]), zs(offs["cu"]), pl.BlockSpec(conv_w.shape, lambda b, t: (0, 0))]
    args = [z, z, z, conv_w]
    if state is not None:
        in_specs.append(st_spec)
        args.append(state)
    return pl.pallas_call(
        functools.partial(_conv_kernel, sb=sb, rows_per_seq=ln, has_state=state is not None),
        out_shape=(jax.ShapeDtypeStruct((nseq * seq_len, CONV_D), BF),
                   jax.ShapeDtypeStruct((nseq, CONV_W - 1, CONV_D), F32)),
        grid=(nseq // sb, nt),
        in_specs=in_specs,
        out_specs=(pl.BlockSpec((rows, CONV_D), lambda b, t: (b * nt + t, 0)), st_spec),
        scratch_shapes=[pltpu.VMEM((sb, 8, CONV_D), F32)],
        compiler_params=_cparams(("parallel", "arbitrary")),
        name="short_conv",
    )(*args)


def _merge_kernel(oa_ref, ob_ref, oc_ref, wa_ref, wb_ref, wc_ref, ga_ref, gb_ref, gc_ref, o_ref):
    def br(o, w, g):
        return jax.nn.sigmoid(g[...]) * jnp.dot(o[...], w[...].astype(BF), preferred_element_type=F32)

    m = br(oa_ref, wa_ref, ga_ref) + br(ob_ref, wb_ref, gb_ref) + br(oc_ref, wc_ref, gc_ref)
    o_ref[...] = m.astype(o_ref.dtype)


def _merge(oa, ob, oc, wa, wb, wc, z, offs, d):
    t = oa.shape[0]
    tm = _pick(t, 1024, 16)
    tn = _pick(d, 512, LANES)
    osp = lambda a: pl.BlockSpec((tm, a.shape[1]), lambda i, j: (i, 0))
    wsp = lambda w: pl.BlockSpec((w.shape[0], tn), lambda i, j: (0, j))
    gsp = lambda off: pl.BlockSpec((tm, tn), lambda i, j: (i, off // tn + j))
    return pl.pallas_call(
        _merge_kernel,
        out_shape=jax.ShapeDtypeStruct((t, d), BF),
        grid=(t // tm, d // tn),
        in_specs=[osp(oa), osp(ob), osp(oc), wsp(wa), wsp(wb), wsp(wc),
                  gsp(offs["ga"]), gsp(offs["gb"]), gsp(offs["gc"])],
        out_specs=pl.BlockSpec((tm, tn), lambda i, j: (i, j)),
        compiler_params=_cparams(("parallel", "arbitrary")),
        name="branch_merge",
    )(oa, ob, oc, wa, wb, wc, z, z, z)


def _norm_router_kernel(x_ref, g_ref, whi_ref, wlo_ref, h_ref, lg_ref):
    y = _rms(x_ref[...], g_ref[...])
    h_ref[...] = y
    yhi = y.astype(BF)
    ylo = (y - yhi.astype(F32)).astype(BF)
    whi = whi_ref[...]
    lg_ref[...] = (jnp.dot(yhi, whi, preferred_element_type=F32)
                   + (jnp.dot(ylo, whi, preferred_element_type=F32)
                      + jnp.dot(yhi, wlo_ref[...], preferred_element_type=F32)))


def _norm_router(x, g, whi, wlo):
    t, d = x.shape
    tm = _pick(t, 512)
    n = whi.shape[1]
    return pl.pallas_call(
        _norm_router_kernel,
        out_shape=(jax.ShapeDtypeStruct((t, d), F32), jax.ShapeDtypeStruct((t, n), F32)),
        grid=(t // tm,),
        in_specs=[pl.BlockSpec((tm, d), lambda i: (i, 0)), pl.BlockSpec((1, d), lambda i: (0, 0)),
                  pl.BlockSpec((d, n), lambda i: (0, 0)), pl.BlockSpec((d, n), lambda i: (0, 0))],
        out_specs=(pl.BlockSpec((tm, d), lambda i: (i, 0)), pl.BlockSpec((tm, n), lambda i: (i, 0))),
        compiler_params=_cparams(("parallel",)),
        name="norm_router",
    )(x, g.reshape(1, d), whi, wlo)


def _row_copy(src_hbm, dst, sem, src_row, dst_slot, dst_row):
    return pltpu.make_async_copy(src_hbm.at[pl.ds(src_row, 1), :],
                                 dst.at[dst_slot, pl.ds(dst_row, 1), :], sem.at[dst_slot])


def _gather_rows(idx_ref, src_hbm, dst, sem, slot, n):
    def body(r, carry):
        _row_copy(src_hbm, dst, sem, idx_ref[0, r], slot, r).start()
        return carry
    lax.fori_loop(0, n, body, 0, unroll=8)


def _wait_rows(src_hbm, dst, sem, slot, n):
    def body(r, carry):
        _row_copy(src_hbm, dst, sem, 0, slot, r).wait()
        return carry
    lax.fori_loop(0, n, body, 0, unroll=8)


def _expert_kernel(be_ref, nu_ref, tok_ref, tok_next_ref, h_hbm, sw_ref, wg_ref, wu_ref, wd_ref, y_ref,
                   xbuf, sem, wg_sc, wu_sc, wd_sc):
    b = pl.program_id(0)
    nu = nu_ref[0]
    slot = b % 2
    bm = xbuf.shape[1]

    @pl.when(b == 0)
    def _():
        _gather_rows(tok_ref, h_hbm, xbuf, sem, 0, bm)

    @pl.when(b + 1 < nu)
    def _():
        _gather_rows(tok_next_ref, h_hbm, xbuf, sem, 1 - slot, bm)

    @pl.when(b < nu)
    def _():
        _wait_rows(h_hbm, xbuf, sem, slot, bm)

        @pl.when((b == 0) | (be_ref[b] != be_ref[jnp.maximum(b - 1, 0)]))
        def _():
            wg_sc[...] = wg_ref[...].astype(BF)
            wu_sc[...] = wu_ref[...].astype(BF)
            wd_sc[...] = wd_ref[...].astype(BF)

        x = xbuf[slot].astype(BF)
        a = jnp.dot(x, wg_sc[...], preferred_element_type=F32)
        u = jnp.dot(x, wu_sc[...], preferred_element_type=F32)
        hid = (a * jax.nn.sigmoid(a) * u).astype(BF)
        y_ref[...] = jnp.dot(hid, wd_sc[...], preferred_element_type=F32) * sw_ref[...]

    @pl.when(b >= nu)
    def _():
        y_ref[...] = jnp.zeros(y_ref.shape, F32)


def _experts(hn, slot_tok, slot_w, block_expert, n_used, wg, wu, wd):
    n_blocks = block_expert.shape[0]
    bm = MOE_BM
    d = hn.shape[1]
    ff = wg.shape[2]
    tok3 = slot_tok.reshape(n_blocks, 1, bm)
    smem_blk = lambda f: pl.BlockSpec((None, 1, bm), f, memory_space=pltpu.SMEM)
    return pl.pallas_call(
        _expert_kernel,
        out_shape=jax.ShapeDtypeStruct((n_blocks * bm, d), F32),
        grid_spec=pltpu.PrefetchScalarGridSpec(
            num_scalar_prefetch=2,
            grid=(n_blocks,),
            in_specs=[smem_blk(lambda b, be, nu: (b, 0, 0)),
                      smem_blk(lambda b, be, nu: (jnp.minimum(b + 1, n_blocks - 1), 0, 0)),
                      pl.BlockSpec(memory_space=pl.ANY),
                      pl.BlockSpec((bm, 1), lambda b, be, nu: (b, 0)),
                      pl.BlockSpec((None, d, ff), lambda b, be, nu: (be[b], 0, 0)),
                      pl.BlockSpec((None, d, ff), lambda b, be, nu: (be[b], 0, 0)),
                      pl.BlockSpec((None, ff, d), lambda b, be, nu: (be[b], 0, 0))],
            out_specs=pl.BlockSpec((bm, d), lambda b, be, nu: (b, 0)),
            scratch_shapes=[pltpu.VMEM((2, bm, d), F32), pltpu.SemaphoreType.DMA((2,)),
                            pltpu.VMEM((d, ff), BF), pltpu.VMEM((d, ff), BF), pltpu.VMEM((ff, d), BF)]),
        compiler_params=_cparams(("arbitrary",)),
        name="moe_experts",
    )(block_expert, n_used, tok3, tok3, hn, slot_w.reshape(-1, 1), wg, wu, wd)


def _combine_kernel(d_ref, d_next_ref, x_ref, y_hbm, o_ref, ybuf, sem):
    i = pl.program_id(0)
    slot = i % 2
    n = ybuf.shape[1]
    tm = n // TOP_K

    @pl.when(i == 0)
    def _():
        _gather_rows(d_ref, y_hbm, ybuf, sem, 0, n)

    @pl.when(i + 1 < pl.num_programs(0))
    def _():
        _gather_rows(d_next_ref, y_hbm, ybuf, sem, 1 - slot, n)

    _wait_rows(y_hbm, ybuf, sem, slot, n)
    o_ref[...] = x_ref[...] + (ybuf[slot, 0:tm, :] + ybuf[slot, tm:n, :])


def _combine(x, y, dest):
    t, d = x.shape
    tm = _pick(t, 128)
    nt = t // tm
    d3 = dest.reshape(nt, tm, TOP_K).transpose(0, 2, 1).reshape(nt, 1, TOP_K * tm)
    smem_blk = lambda f: pl.BlockSpec((None, 1, TOP_K * tm), f, memory_space=pltpu.SMEM)
    return pl.pallas_call(
        _combine_kernel,
        out_shape=jax.ShapeDtypeStruct((t, d), F32),
        grid=(nt,),
        in_specs=[smem_blk(lambda i: (i, 0, 0)),
                  smem_blk(lambda i: (jnp.minimum(i + 1, nt - 1), 0, 0)),
                  pl.BlockSpec((tm, d), lambda i: (i, 0)),
                  pl.BlockSpec(memory_space=pl.ANY)],
        out_specs=pl.BlockSpec((tm, d), lambda i: (i, 0)),
        scratch_shapes=[pltpu.VMEM((2, TOP_K * tm, d), F32), pltpu.SemaphoreType.DMA((2,))],
        compiler_params=_cparams(("arbitrary",)),
        name="moe_combine",
    )(d3, d3, x, y)


def _route(logits, bias_g, bias_e):
    t = logits.shape[0]
    g_logits = logits[:, :N_GROUPS] + bias_g
    e_logits = (logits[:, N_GROUPS:N_GROUPS + N_EXPERTS] + bias_e).reshape(t, N_GROUPS, EXPERTS_PER_GROUP)
    p_group = jax.nn.softmax(g_logits, axis=-1)
    grp = jnp.argmax(g_logits, axis=-1).astype(jnp.int32)
    in_grp = jnp.take_along_axis(e_logits, grp[:, None, None], axis=1)[:, 0]
    top_v, top_i = lax.top_k(in_grp, TOP_K)
    gate = jax.nn.softmax(top_v, axis=-1) * jnp.take_along_axis(p_group, grp[:, None], axis=1)
    expert = grp[:, None] * EXPERTS_PER_GROUP + top_i.astype(jnp.int32)

    bm = MOE_BM
    n_assign = t * TOP_K
    e_flat = expert.reshape(-1)
    onehot = (e_flat[:, None] == jnp.arange(N_EXPERTS, dtype=jnp.int32)[None, :]).astype(jnp.int32)
    rank_all = jnp.cumsum(onehot, axis=0)
    counts = rank_all[-1]
    rank = jnp.take_along_axis(rank_all, e_flat[:, None], axis=1)[:, 0] - 1
    padded = (counts + bm - 1) // bm * bm
    pad_end = jnp.cumsum(padded)
    pad_start = pad_end - padded
    dest = (pad_start[e_flat] + rank).astype(jnp.int32)
    n_blocks = (n_assign + N_EXPERTS * (bm - 1) + bm - 1) // bm
    n_slots = n_blocks * bm
    tok = jnp.repeat(jnp.arange(t, dtype=jnp.int32), TOP_K)
    slot_tok = jnp.zeros((n_slots,), jnp.int32).at[dest].set(tok)
    slot_w = jnp.zeros((n_slots,), F32).at[dest].set(gate.reshape(-1))
    block_expert = jnp.minimum(
        jnp.searchsorted(pad_end, jnp.arange(n_blocks, dtype=jnp.int32) * bm, side="right"),
        N_EXPERTS - 1).astype(jnp.int32)
    n_used = (pad_end[-1] // bm).astype(jnp.int32).reshape(1)
    return slot_tok, slot_w, block_expert, n_used, dest.reshape(t, TOP_K)


def _rope_tables(pos, half):
    inv_freq = ROPE_BASE ** (-jnp.arange(half, dtype=F32) / half)
    ang = pos.astype(F32)[:, None] * inv_freq[None, :]
    return jnp.cos(ang), jnp.sin(ang)


def _z_layout(d):
    segs = [("ga", d), ("gb", d), ("gc", d),
            ("rv", RET_HEADS * RET_DV), ("rg", RET_HEADS * RET_DV),
            ("cb", CONV_D), ("cc", CONV_D), ("cu", CONV_D),
            ("cq", Q_LORA), ("rq", RET_HEADS * RET_DK), ("rk", RET_HEADS * RET_DK),
            ("ckv", KV_LORA), ("kpe", LANES)]
    offs, o = {}, 0
    for name, w in segs:
        assert o % w == 0
        offs[name] = o
        o += w
    return segs, offs, o


def _permute_w_in(w_in, d):
    src_names = ["cq", "ckv", "kpe", "rq", "rk", "rv", "rg", "cb", "cc", "cu", "ga", "gb", "gc"]
    src_w = [Q_LORA, KV_LORA, MLA_ROPE, RET_HEADS * RET_DK, RET_HEADS * RET_DK, RET_HEADS * RET_DV,
             RET_HEADS * RET_DV, CONV_D, CONV_D, CONV_D, d, d, d]
    src_off = dict(zip(src_names, np.cumsum([0] + src_w[:-1]).tolist()))
    src_width = dict(zip(src_names, src_w))
    segs, offs, total = _z_layout(d)
    parts = []
    for name, w in segs:
        p = w_in[:, src_off[name]: src_off[name] + src_width[name]]
        if src_width[name] < w:
            p = jnp.pad(p, ((0, 0), (0, w - src_width[name])))
        parts.append(p)
    return jnp.concatenate(parts, axis=1), offs


def kernel(x_prompt, x_sample, cache_ckv, cache_kpe, state_ret, state_conv, page_table, norm_mix_g, w_in, mla_g_cq, mla_g_ckv, mla_w_uq, mla_w_uk, mla_w_uv, mla_g_qn, mla_g_kn, ret_g, conv_w, w_branch_a, w_branch_b, w_branch_c, w_out, norm_ffn_g, router_group_w, router_group_b, router_expert_w, router_expert_b, expert_w_gate, expert_w_up, expert_w_down):
    nb, s, d = x_prompt.shape
    ns, dq, _ = x_sample.shape
    depth = w_in.shape[0]
    tp, ts = nb * s, ns * dq
    past_len = page_table.shape[1] * PAGE_SIZE

    x = jnp.concatenate([x_prompt.reshape(tp, d), x_sample.reshape(ts, d)], axis=0)
    pos = jnp.concatenate([jnp.tile(jnp.arange(s, dtype=jnp.int32), nb),
                           jnp.tile(past_len + jnp.arange(dq, dtype=jnp.int32), ns)])
    cm, sm = _rope_tables(pos, MLA_ROPE // 2)
    zm = jnp.zeros_like(sm)
    mla_tabs = (jnp.concatenate([cm, cm, cm, cm], axis=1),
                jnp.concatenate([-sm, zm, -sm, zm], axis=1),
                jnp.concatenate([zm, sm, zm, sm], axis=1))
    cr, sr = _rope_tables(pos, RET_DK // 2)
    ret_tabs = (jnp.concatenate([cr, cr], axis=1), jnp.concatenate([-sr, sr], axis=1))

    uq_perm = np.concatenate(
        [h * MLA_QK + np.arange(MLA_NOPE) for h in range(MLA_HEADS)]
        + [h * MLA_QK + MLA_NOPE + np.arange(MLA_ROPE) for h in range(MLA_HEADS)])

    outs = {k: [] for k in ("ckv_p", "kpe_p", "ret_p", "conv_p", "ckv_s", "kpe_s", "ret_s", "conv_s")}
    for l in range(depth):
        w_in_p, offs = _permute_w_in(w_in[l], d)
        wuq_p = mla_w_uq[l][:, uq_perm].astype(BF)
        wukt = mla_w_uk[l].T.astype(BF)
        wuv = mla_w_uv[l].astype(BF)
        gqn = mla_g_qn[l][:MLA_NOPE].reshape(1, -1)
        gqr = jnp.tile(mla_g_qn[l][MLA_NOPE:], 2).reshape(1, -1)
        gkn = mla_g_kn[l][:MLA_NOPE].reshape(1, -1)
        gkr = jnp.tile(mla_g_kn[l][MLA_NOPE:], 2).reshape(1, -1)

        xn = _norm_cast(x, norm_mix_g[l])
        z = _matmul(xn, w_in_p, name="in_proj")

        c_lat, k_rot, q, kcat, rs = _mla_prep(
            z, offs, mla_tabs, mla_g_cq[l].reshape(1, -1), mla_g_ckv[l].reshape(1, -1),
            wuq_p, wukt, gqn, gqr, gkn, gkr)
        oa_p = _attn_prompt(q, kcat, rs, wuv, nb, s)
        oa_s = _attn_paged(page_table, q, c_lat[tp:].reshape(ns, dq, KV_LORA),
                           k_rot[tp:].reshape(ns, dq, MLA_ROPE), wukt, wuv,
                           cache_ckv, cache_kpe, l, tp // dq)
        oa = jnp.concatenate([oa_p, oa_s.reshape(ts, -1).astype(BF)], axis=0)

        ob_p, ret_p = _retention(z, offs, ret_tabs, ret_g[l], nb, s, 0, None)
        ob_s, ret_s = _retention(z, offs, ret_tabs, ret_g[l], ns, dq, tp, state_ret[l])
        ob = jnp.concatenate([ob_p, ob_s], axis=0)

        oc_p, conv_p = _short_conv(z, offs, conv_w[l], nb, s, 0, None)
        oc_s, conv_s = _short_conv(z, offs, conv_w[l], ns, dq, tp, state_conv[l])
        oc = jnp.concatenate([oc_p, oc_s], axis=0)

        m = _merge(oa, ob, oc, w_branch_a[l], w_branch_b[l], w_branch_c[l], z, offs, d)
        x = _matmul(m, w_out[l], residual=x, tm_target=1024, tn_target=512, name="out_proj")

        wr = jnp.concatenate([router_group_w[l], router_expert_w[l]], axis=1)
        wr = jnp.pad(wr, ((0, 0), (0, LANES - wr.shape[1])))
        wr_hi = wr.astype(BF)
        wr_lo = (wr - wr_hi.astype(F32)).astype(BF)
        hn, logits = _norm_router(x, norm_ffn_g[l], wr_hi, wr_lo)
        slot_tok, slot_w, block_expert, n_used, dest = _route(logits, router_group_b[l], router_expert_b[l])
        y = _experts(hn, slot_tok, slot_w, block_expert, n_used,
                     expert_w_gate[l], expert_w_up[l], expert_w_down[l])
        x = _combine(x, y, dest)

        outs["ckv_p"].append(c_lat[:tp].reshape(nb, s, KV_LORA))
        outs["kpe_p"].append(k_rot[:tp].reshape(nb, s, MLA_ROPE))
        outs["ret_p"].append(ret_p)
        outs["conv_p"].append(conv_p)
        outs["ckv_s"].append(c_lat[tp:].reshape(ns, dq, KV_LORA))
        outs["kpe_s"].append(k_rot[tp:].reshape(ns, dq, MLA_ROPE))
        outs["ret_s"].append(ret_s)
        outs["conv_s"].append(conv_s)

    st = lambda k: jnp.stack(outs[k])
    return (x[:tp].reshape(nb, s, d), x[tp:].reshape(ns, dq, d),
            st("ckv_p"), st("kpe_p"), st("ret_p"), st("conv_p"),
            st("ckv_s"), st("kpe_s"), st("ret_s"), st("conv_s"))
```

```python
import functools
import math

import numpy as np
import jax
import jax.numpy as jnp
from jax import lax
from jax.experimental import pallas as pl
from jax.experimental.pallas import tpu as pltpu

BF = jnp.bfloat16
F32 = jnp.float32

MLA_HEADS = 8
MLA_NOPE = 128
MLA_ROPE = 64
MLA_QK = MLA_NOPE + MLA_ROPE
MLA_V = 128
Q_LORA = 512
KV_LORA = 128
RET_HEADS = 4
RET_DK = 128
RET_DV = 256
RET_CHUNK = 128
CONV_D = 1024
CONV_W = 3
N_GROUPS = 8
EXPERTS_PER_GROUP = 8
N_EXPERTS = N_GROUPS * EXPERTS_PER_GROUP
TOP_K = 2
EXPERT_FF = 512
PAGE_SIZE = 128
ROPE_BASE = 10000.0
NORM_EPS = 1e-6
ATTN_SCALE = MLA_QK ** -0.5
LOG_G = tuple(math.log1p(-(2.0 ** (-5.0 - h))) for h in range(RET_HEADS))

LANES = 128
QPAD = 256
VMEM_LIMIT = 56 * 1024 * 1024
MOE_BM = 256
NEG_INF = float("-inf")

_NT = (((1,), (1,)), ((), ()))


def _cparams(sem, vmem=VMEM_LIMIT):
    return pltpu.CompilerParams(dimension_semantics=sem, vmem_limit_bytes=vmem)


def _pick(n, target, mult=8):
    if n <= target:
        return n
    for d in range(target, 0, -1):
        if n % d == 0 and d % mult == 0:
            return d
    return n


def _rms(x, g):
    return x * lax.rsqrt(jnp.mean(x * x, axis=-1, keepdims=True) + NORM_EPS) * g


def _lane_sumsq_t(x):
    sq = x * x
    hi = sq.astype(BF)
    r1 = sq - hi.astype(F32)
    mid = r1.astype(BF)
    lo = (r1 - mid.astype(F32)).astype(BF)
    ones = jnp.ones((8, x.shape[1]), BF)
    nt = lambda b: lax.dot_general(ones, b, _NT, preferred_element_type=F32)
    return nt(hi) + (nt(mid) + nt(lo))


def _norm_kernel(x_ref, g_ref, o_ref):
    o_ref[...] = _rms(x_ref[...], g_ref[...]).astype(o_ref.dtype)


def _norm_cast(x, g):
    t, d = x.shape
    tm = _pick(t, 512)
    return pl.pallas_call(
        _norm_kernel,
        out_shape=jax.ShapeDtypeStruct((t, d), BF),
        grid=(t // tm,),
        in_specs=[pl.BlockSpec((tm, d), lambda i: (i, 0)), pl.BlockSpec((1, d), lambda i: (0, 0))],
        out_specs=pl.BlockSpec((tm, d), lambda i: (i, 0)),
        compiler_params=_cparams(("parallel",)),
        name="norm_cast",
    )(x, g.reshape(1, d))


def _mm_kernel(a_ref, w_ref, o_ref):
    o_ref[...] = jnp.dot(a_ref[...].astype(BF), w_ref[...].astype(BF),
                         preferred_element_type=F32).astype(o_ref.dtype)


def _mm_res_kernel(a_ref, w_ref, r_ref, o_ref):
    o_ref[...] = r_ref[...] + jnp.dot(a_ref[...].astype(BF), w_ref[...].astype(BF),
                                      preferred_element_type=F32)


def _in_proj_kernel(offs_ref, a_ref, w_ref, o_ref):
    del offs_ref
    o_ref[...] = lax.dot_general(a_ref[...], w_ref[0].astype(BF), _NT, preferred_element_type=F32)


def _in_proj(xn, w_in_t, layer, tile_rows, tn):
    m, k = xn.shape
    tm = _pick(m, 2304, 16)
    nt = len(tile_rows)
    unit = math.gcd(*tile_rows, tn)
    assert unit % 8 == 0
    offs = jnp.asarray([r // unit for r in tile_rows], jnp.int32)
    return pl.pallas_call(
        _in_proj_kernel,
        out_shape=jax.ShapeDtypeStruct((m, nt * tn), F32),
        grid_spec=pltpu.PrefetchScalarGridSpec(
            num_scalar_prefetch=1,
            grid=(m // tm, nt),
            in_specs=[pl.BlockSpec((tm, k), lambda i, j, offs: (i, 0)),
                      pl.BlockSpec((pl.Element(1), pl.Element(tn), pl.Element(k)),
                                   lambda i, j, offs: (layer, offs[j] * unit, 0))],
            out_specs=pl.BlockSpec((tm, tn), lambda i, j, offs: (i, j))),
        compiler_params=_cparams(("parallel", "arbitrary")),
        name="in_proj",
    )(offs, xn, w_in_t)


def _matmul(a, w3, layer, residual=None, tm_target=2304, tn_target=256, name="matmul"):
    m, k = a.shape
    n = w3.shape[2]
    tm = _pick(m, tm_target, 16)
    tn = _pick(n, tn_target, LANES)
    in_specs = [pl.BlockSpec((tm, k), lambda i, j: (i, 0)),
                pl.BlockSpec((None, k, tn), lambda i, j: (layer, 0, j))]
    args = [a, w3]
    body = _mm_kernel
    if residual is not None:
        in_specs.append(pl.BlockSpec((tm, tn), lambda i, j: (i, j)))
        args.append(residual)
        body = _mm_res_kernel
    return pl.pallas_call(
        body,
        out_shape=jax.ShapeDtypeStruct((m, n), F32),
        grid=(m // tm, n // tn),
        in_specs=in_specs,
        out_specs=pl.BlockSpec((tm, tn), lambda i, j: (i, j)),
        compiler_params=_cparams(("parallel", "arbitrary")),
        name=name,
    )(*args)


def _prep_kernel(cq_ref, ckv_ref, kpe_ref, c_ref, s1_ref, s2_ref, gcq_ref, gckv_ref, wuq_ref, wukt_ref,
                 gqn_ref, gqr_ref, gkn_ref, gkr_ref,
                 clat_ref, krot_ref, q_ref, kcat_ref, rs_ref):
    tm = cq_ref.shape[0]
    cs, s1, s2 = c_ref[...], s1_ref[...], s2_ref[...]

    def rope64(x):
        return x * cs + pltpu.roll(x, 96, 1) * s1 + pltpu.roll(x, 32, 1) * s2

    low = lax.broadcasted_iota(jnp.int32, (tm, LANES), 1) < MLA_ROPE

    c_lat = _rms(ckv_ref[...], gckv_ref[...])
    clat_ref[...] = c_lat
    k_rot = jnp.where(low, rope64(kpe_ref[...]), 0.0)
    krot_ref[...] = k_rot[:, :MLA_ROPE]
    clb = c_lat.astype(BF)
    kcat_ref[:, 0:KV_LORA] = clb
    kcat_ref[:, KV_LORA:QPAD] = k_rot.astype(BF)

    knt = lax.dot_general(wukt_ref[...], clb, _NT, preferred_element_type=F32)
    ssq = jnp.sum((knt * knt).reshape(MLA_HEADS, MLA_NOPE, tm), axis=1)
    rs_ref[...] = lax.rsqrt((ssq + _lane_sumsq_t(k_rot)) / MLA_QK + NORM_EPS)

    cqn = _rms(cq_ref[...], gcq_ref[...]).astype(BF)
    q = jnp.dot(cqn, wuq_ref[...], preferred_element_type=F32)
    gqn, gqr, gkn, gkr = gqn_ref[...], gqr_ref[...], gkn_ref[...], gkr_ref[...]
    nope_w = MLA_HEADS * MLA_NOPE
    for j in range(MLA_HEADS // 2):
        xr = rope64(q[:, nope_w + LANES * j: nope_w + LANES * (j + 1)])
        sq = xr * xr
        ss = (jnp.sum(jnp.where(low, sq, 0.0), axis=-1, keepdims=True),
              jnp.sum(jnp.where(low, 0.0, sq), axis=-1, keepdims=True))
        for e in range(2):
            h = 2 * j + e
            nope = q[:, MLA_NOPE * h: MLA_NOPE * (h + 1)]
            r = lax.rsqrt((jnp.sum(nope * nope, axis=-1, keepdims=True) + ss[e]) / MLA_QK + NORM_EPS)
            qg = (nope * r * gqn * gkn).astype(BF)
            q_ref[h, :, 0:KV_LORA] = jnp.dot(qg, wukt_ref[MLA_NOPE * h: MLA_NOPE * (h + 1), :],
                                             preferred_element_type=F32)
            rp = xr * r * gqr * gkr
            if e == 1:
                rp = pltpu.roll(rp, MLA_ROPE, 1)
            q_ref[h, :, KV_LORA:QPAD] = jnp.where(low, rp, 0.0)


def _mla_prep(z, offs, tabs, gcq, gckv, wuq_p, wukt, gqn, gqr, gkn, gkr):
    t = z.shape[0]
    tm = _pick(t, 256, 16)
    cs, s1, s2 = tabs

    def zs(width, off):
        return pl.BlockSpec((tm, width), lambda i: (i, off // width))

    row = lambda w: pl.BlockSpec((tm, w), lambda i: (i, 0))
    full = lambda a: pl.BlockSpec(a.shape, lambda i: (0,) * a.ndim)
    small = [gcq, gckv, wuq_p, wukt, gqn, gqr, gkn, gkr]
    return pl.pallas_call(
        _prep_kernel,
        out_shape=(jax.ShapeDtypeStruct((t, KV_LORA), F32),
                   jax.ShapeDtypeStruct((t, MLA_ROPE), F32),
                   jax.ShapeDtypeStruct((MLA_HEADS, t, QPAD), F32),
                   jax.ShapeDtypeStruct((t, QPAD), BF),
                   jax.ShapeDtypeStruct((MLA_HEADS, t), F32)),
        grid=(t // tm,),
        in_specs=[zs(Q_LORA, offs["cq"]), zs(KV_LORA, offs["ckv"]), zs(LANES, offs["kpe"]),
                  row(LANES), row(LANES), row(LANES)] + [full(a) for a in small],
        out_specs=(row(KV_LORA), row(MLA_ROPE),
                   pl.BlockSpec((MLA_HEADS, tm, QPAD), lambda i: (0, i, 0)),
                   row(QPAD),
                   pl.BlockSpec((MLA_HEADS, tm), lambda i: (0, i))),
        compiler_params=_cparams(("parallel",)),
        name="mla_prep",
    )(z, z, z, cs, s1, s2, *small)


def _attn_prompt_kernel(q_ref, k_ref, rs_ref, wuv_ref, o_ref, qb_sc, m_sc, l_sc, acc_sc, *, bq, bk):
    qi = pl.program_id(1)
    ki = pl.program_id(2)
    rows = MLA_HEADS * bq

    @pl.when(ki == 0)
    def _():
        qb_sc[...] = q_ref[...].reshape(rows, QPAD).astype(BF)
        m_sc[...] = jnp.full(m_sc.shape, NEG_INF, F32)
        l_sc[...] = jnp.zeros(l_sc.shape, F32)
        acc_sc[...] = jnp.zeros(acc_sc.shape, F32)

    @pl.when(ki * bk <= qi * bq + (bq - 1))
    def _():
        k = k_ref[...]
        s = lax.dot_general(qb_sc[...], k, _NT, preferred_element_type=F32)
        rs = rs_ref[...] * ATTN_SCALE
        s3 = s.reshape(MLA_HEADS, bq, bk) * rs[:, None, :]
        qpos = qi * bq + lax.broadcasted_iota(jnp.int32, (1, bq, bk), 1)
        kpos = ki * bk + lax.broadcasted_iota(jnp.int32, (1, bq, bk), 2)
        s2 = jnp.where(kpos <= qpos, s3, NEG_INF).reshape(rows, bk)
        m_prev = m_sc[...]
        m_new = jnp.maximum(m_prev, jnp.max(s2, axis=-1, keepdims=True))
        alpha = jnp.exp(m_prev - m_new)
        p = jnp.exp(s2 - m_new)
        l_sc[...] = alpha * l_sc[...] + jnp.sum(p, axis=-1, keepdims=True)
        acc_sc[...] = alpha * acc_sc[...] + jnp.dot(p.astype(BF), k[:, 0:KV_LORA],
                                                    preferred_element_type=F32)
        m_sc[...] = m_new

    @pl.when(ki == pl.num_programs(2) - 1)
    def _():
        ol = (acc_sc[...] / l_sc[...]).astype(BF)
        for h in range(MLA_HEADS):
            o_ref[:, MLA_V * h: MLA_V * (h + 1)] = jnp.dot(
                ol[h * bq:(h + 1) * bq], wuv_ref[:, MLA_V * h: MLA_V * (h + 1)],
                preferred_element_type=F32).astype(o_ref.dtype)


def _attn_prompt(q, kcat, rs, wuv, nb, s):
    bq = _pick(s, 256, 16)
    bk = _pick(s, 512, LANES)
    nq, nk = s // bq, s // bk
    rows = MLA_HEADS * bq

    def kblk(b, qi, ki):
        return b * nk + jnp.minimum(ki, (qi * bq + bq - 1) // bk)

    return pl.pallas_call(
        functools.partial(_attn_prompt_kernel, bq=bq, bk=bk),
        out_shape=jax.ShapeDtypeStruct((nb * s, MLA_HEADS * MLA_V), BF),
        grid=(nb, nq, nk),
        in_specs=[pl.BlockSpec((MLA_HEADS, bq, QPAD), lambda b, qi, ki: (0, b * nq + qi, 0)),
                  pl.BlockSpec((bk, QPAD), lambda b, qi, ki: (kblk(b, qi, ki), 0)),
                  pl.BlockSpec((MLA_HEADS, bk), lambda b, qi, ki: (0, kblk(b, qi, ki))),
                  pl.BlockSpec(wuv.shape, lambda b, qi, ki: (0, 0))],
        out_specs=pl.BlockSpec((bq, MLA_HEADS * MLA_V), lambda b, qi, ki: (b * nq + qi, 0)),
        scratch_shapes=[pltpu.VMEM((rows, QPAD), BF), pltpu.VMEM((rows, 1), F32),
                        pltpu.VMEM((rows, 1), F32), pltpu.VMEM((rows, KV_LORA), F32)],
        compiler_params=_cparams(("parallel", "parallel", "arbitrary")),
        name="attn_prompt",
    )(q, kcat, rs, wuv)


def _attn_paged_kernel(pt_ref, q_ref, cnew_ref, krnew_ref, wukt_ref, wuv_ref, *rest, pp, dq):
    c_refs = rest[:pp]
    kr_refs = rest[pp:2 * pp]
    o_ref = rest[2 * pp]
    wq_sc, qr_sc, m_sc, l_sc, acc_sc, cpad_sc, krpad_sc = rest[2 * pp + 1:]
    j = pl.program_id(1)
    rows = MLA_HEADS * dq
    nk = MLA_HEADS * MLA_NOPE

    @pl.when(j == 0)
    def _():
        q = q_ref[...].reshape(rows, QPAD)
        wq_sc[0:nk, :] = wukt_ref[...]
        wq_sc[nk:nk + rows, :] = q[:, 0:KV_LORA].astype(BF)
        qr_sc[...] = q[:, KV_LORA:KV_LORA + MLA_ROPE].astype(BF)
        m_sc[...] = jnp.full(m_sc.shape, NEG_INF, F32)
        l_sc[...] = jnp.zeros(l_sc.shape, F32)
        acc_sc[...] = jnp.zeros(acc_sc.shape, F32)

    def update(c, krt, mask):
        n = c.shape[0]
        cb = c.astype(BF)
        big = lax.dot_general(wq_sc[...], cb, _NT, preferred_element_type=F32)
        knt = big[0:nk]
        ssq = jnp.sum((knt * knt).reshape(MLA_HEADS, MLA_NOPE, n), axis=1)
        krsq = jnp.sum(krt * krt, axis=0, keepdims=True)
        rs = lax.rsqrt((ssq + krsq) / MLA_QK + NORM_EPS) * ATTN_SCALE
        s = big[nk:nk + rows] + jnp.dot(qr_sc[...], krt.astype(BF), preferred_element_type=F32)
        s = (s.reshape(MLA_HEADS, dq, n) * rs[:, None, :]).reshape(rows, n)
        if mask is not None:
            s = jnp.where(mask, s, NEG_INF)
        m_prev = m_sc[...]
        m_new = jnp.maximum(m_prev, jnp.max(s, axis=-1, keepdims=True))
        alpha = jnp.exp(m_prev - m_new)
        p = jnp.exp(s - m_new)
        l_sc[...] = alpha * l_sc[...] + jnp.sum(p, axis=-1, keepdims=True)
        acc_sc[...] = alpha * acc_sc[...] + jnp.dot(p.astype(BF), cb, preferred_element_type=F32)
        m_sc[...] = m_new

    update(jnp.concatenate([r[...] for r in c_refs], axis=0),
           jnp.concatenate([r[...] for r in kr_refs], axis=1), None)

    @pl.when(j == pl.num_programs(1) - 1)
    def _():
        cpad_sc[...] = jnp.zeros(cpad_sc.shape, F32)
        krpad_sc[...] = jnp.zeros(krpad_sc.shape, F32)
        cpad_sc[0:dq, :] = cnew_ref[...]
        krpad_sc[:, 0:dq] = krnew_ref[...]
        t = lax.broadcasted_iota(jnp.int32, (rows, PAGE_SIZE), 1)
        r = lax.broadcasted_iota(jnp.int32, (MLA_HEADS, dq, PAGE_SIZE), 1).reshape(rows, PAGE_SIZE)
        update(cpad_sc[...], krpad_sc[...], t <= r)
        ol = (acc_sc[...] / l_sc[...]).astype(BF)
        full = jnp.dot(ol, wuv_ref[...], preferred_element_type=F32)
        for h in range(MLA_HEADS):
            o_ref[:, MLA_V * h: MLA_V * (h + 1)] = full[h * dq:(h + 1) * dq, MLA_V * h: MLA_V * (h + 1)]


PAGES_PER_STEP = 16


def _attn_paged(page_table, q, c_new, kr_new_t, wukt, wuv, cache_ckv, cache_kpe_t, layer, row_blk0):
    nseq, n_pages = page_table.shape
    dq = c_new.shape[1]
    pp = math.gcd(n_pages, PAGES_PER_STEP)
    n_chunks = n_pages // pp
    pt = page_table.reshape(-1)

    def cache_spec(shape, i):
        return pl.BlockSpec((None, None) + shape,
                            lambda b, j, pt_ref: (layer, pt_ref[b * n_pages + j * pp + i], 0, 0))

    in_specs = [pl.BlockSpec((MLA_HEADS, dq, QPAD), lambda b, j, pt_ref: (0, row_blk0 + b, 0)),
                pl.BlockSpec((None, dq, KV_LORA), lambda b, j, pt_ref: (b, 0, 0)),
                pl.BlockSpec((None, MLA_ROPE, dq), lambda b, j, pt_ref: (b, 0, 0)),
                pl.BlockSpec(wukt.shape, lambda b, j, pt_ref: (0, 0)),
                pl.BlockSpec(wuv.shape, lambda b, j, pt_ref: (0, 0))]
    in_specs += [cache_spec((PAGE_SIZE, KV_LORA), i) for i in range(pp)]
    in_specs += [cache_spec((MLA_ROPE, PAGE_SIZE), i) for i in range(pp)]
    rows = MLA_HEADS * dq
    return pl.pallas_call(
        functools.partial(_attn_paged_kernel, pp=pp, dq=dq),
        out_shape=jax.ShapeDtypeStruct((nseq, dq, MLA_HEADS * MLA_V), F32),
        grid_spec=pltpu.PrefetchScalarGridSpec(
            num_scalar_prefetch=1,
            grid=(nseq, n_chunks),
            in_specs=in_specs,
            out_specs=pl.BlockSpec((None, dq, MLA_HEADS * MLA_V), lambda b, j, pt_ref: (b, 0, 0)),
            scratch_shapes=[pltpu.VMEM((MLA_HEADS * MLA_NOPE + rows, KV_LORA), BF),
                            pltpu.VMEM((rows, MLA_ROPE), BF),
                            pltpu.VMEM((rows, 1), F32), pltpu.VMEM((rows, 1), F32),
                            pltpu.VMEM((rows, KV_LORA), F32),
                            pltpu.VMEM((PAGE_SIZE, KV_LORA), F32), pltpu.VMEM((MLA_ROPE, PAGE_SIZE), F32)]),
        compiler_params=_cparams(("parallel", "arbitrary")),
        name="attn_paged",
    )(pt, q, c_new, kr_new_t, wukt, wuv, *([cache_ckv] * pp), *([cache_kpe_t] * pp))


def _ret_kernel(rq_ref, rk_ref, rv_ref, rg_ref, cf_ref, sf_ref, g_ref, *rest, chunk, sb, has_state):
    if has_state:
        st_ref, o_ref, so_ref, r_sc, kp_sc, kd_sc, vp_sc, o_sc = rest
    else:
        o_ref, so_ref, r_sc, kp_sc, kd_sc, vp_sc, o_sc = rest
    ci = pl.program_id(1)
    lp = kp_sc.shape[0]

    @pl.when(ci == 0)
    def _():
        if has_state:
            r_sc[...] = st_ref[...]
        else:
            r_sc[...] = jnp.zeros(r_sc.shape, F32)

    if chunk < lp:
        kp_sc[...] = jnp.zeros(kp_sc.shape, F32)
        kd_sc[...] = jnp.zeros(kd_sc.shape, F32)
        vp_sc[...] = jnp.zeros(vp_sc.shape, F32)

    def pad(x, sc):
        if chunk == lp:
            return x
        sc[0:chunk, :] = x
        return sc[...]

    t = lax.broadcasted_iota(jnp.int32, (chunk, 1), 0).astype(F32)
    rel = (lax.broadcasted_iota(jnp.int32, (chunk, lp), 0)
           - lax.broadcasted_iota(jnp.int32, (chunk, lp), 1)).astype(F32)
    for si in range(sb):
        rows = slice(si * chunk, (si + 1) * chunk)
        cf, sf = cf_ref[rows, :], sf_ref[rows, :]
        rope = lambda x: x * cf + pltpu.roll(x, RET_DK // 2, 1) * sf
        for h in range(RET_HEADS):
            lg = LOG_G[h]
            q = rope(rq_ref[rows, RET_DK * h: RET_DK * (h + 1)])
            k = rope(rk_ref[rows, RET_DK * h: RET_DK * (h + 1)]) * (RET_DK ** -0.5)
            v = rv_ref[rows, RET_DV * h: RET_DV * (h + 1)]
            qb = q.astype(BF)
            kp = pad(k, kp_sc).astype(BF)
            vp = pad(v, vp_sc).astype(BF)
            decay = jnp.where(rel >= 0.0, jnp.exp(lg * jnp.maximum(rel, 0.0)), 0.0)
            s = lax.dot_general(qb, kp, _NT, preferred_element_type=F32) * decay
            r_old = r_sc[si, h]
            o = jnp.dot(s.astype(BF), vp, preferred_element_type=F32)
            o = o + jnp.dot(qb, r_old.astype(BF), preferred_element_type=F32) * jnp.exp(lg * (t + 1.0))
            kd = pad(k * jnp.exp(lg * (chunk - 1.0 - t)), kd_sc)
            r_sc[si, h] = math.exp(lg * chunk) * r_old + jnp.dot(
                kd.T.astype(BF), vp, preferred_element_type=F32)
            on = _rms(o, g_ref[h:h + 1, :])
            rg = rg_ref[rows, RET_DV * h: RET_DV * (h + 1)]
            o_sc[rows, RET_DV * h: RET_DV * (h + 1)] = rg * jax.nn.sigmoid(rg) * on
    o_ref[...] = o_sc[...].astype(o_ref.dtype)

    @pl.when(ci == pl.num_programs(1) - 1)
    def _():
        so_ref[...] = r_sc[...]


def _retention(z, offs, tabs, ret_g, nseq, seq_len, row0, state):
    chunk = RET_CHUNK if seq_len % RET_CHUNK == 0 else seq_len
    nc = seq_len // chunk
    sb = 1 if nc > 1 else _pick(nseq, max(1, 64 // chunk), 1)
    rows = sb * chunk
    lp = max(chunk, LANES)
    blk0 = row0 // rows
    cf, sf = tabs
    dk, dv = RET_HEADS * RET_DK, RET_HEADS * RET_DV

    def zs(width, off):
        return pl.BlockSpec((rows, width), lambda b, c: (blk0 + b * nc + c, off // width))

    tab = pl.BlockSpec((rows, RET_DK), lambda b, c: (blk0 + b * nc + c, 0))
    st_spec = pl.BlockSpec((sb, RET_HEADS, RET_DK, RET_DV), lambda b, c: (b, 0, 0, 0))
    in_specs = [zs(dk, offs["rq"]), zs(dk, offs["rk"]), zs(dv, offs["rv"]), zs(dv, offs["rg"]), tab, tab,
                pl.BlockSpec(ret_g.shape, lambda b, c: (0, 0))]
    args = [z, z, z, z, cf, sf, ret_g]
    if state is not None:
        in_specs.append(st_spec)
        args.append(state)
    return pl.pallas_call(
        functools.partial(_ret_kernel, chunk=chunk, sb=sb, has_state=state is not None),
        out_shape=(jax.ShapeDtypeStruct((nseq * seq_len, dv), BF),
                   jax.ShapeDtypeStruct((nseq, RET_HEADS, RET_DK, RET_DV), F32)),
        grid=(nseq // sb, nc),
        in_specs=in_specs,
        out_specs=(pl.BlockSpec((rows, dv), lambda b, c: (b * nc + c, 0)), st_spec),
        scratch_shapes=[pltpu.VMEM((sb, RET_HEADS, RET_DK, RET_DV), F32),
                        pltpu.VMEM((lp, RET_DK), F32), pltpu.VMEM((lp, RET_DK), F32),
                        pltpu.VMEM((lp, RET_DV), F32), pltpu.VMEM((rows, dv), F32)],
        compiler_params=_cparams(("parallel", "arbitrary")),
        name="retention",
    )(*args)


def _conv_kernel(cb_ref, cc_ref, cu_ref, w_ref, *rest, sb, rows_per_seq, has_state):
    if has_state:
        buf_ref, o_ref, so_ref, carry_sc = rest
    else:
        o_ref, so_ref, carry_sc = rest
    ti = pl.program_id(1)
    ln = rows_per_seq
    cd = cb_ref.shape[1]

    @pl.when(ti == 0)
    def _():
        if has_state:
            carry_sc[:, 6:8, :] = buf_ref[...]
        else:
            carry_sc[...] = jnp.zeros(carry_sc.shape, F32)

    w2 = cc_ref[...] * cu_ref[...]
    w3 = w2.reshape(sb, ln, cd)
    r1 = pltpu.roll(w2, 1, 0).reshape(sb, ln, cd)
    r2 = pltpu.roll(w2, 2, 0).reshape(sb, ln, cd)
    prev = carry_sc[...]
    p1, p2 = prev[:, 7:8, :], prev[:, 6:7, :]
    tpos = lax.broadcasted_iota(jnp.int32, (sb, ln, cd), 1)
    x1 = jnp.where(tpos == 0, p1, r1)
    x2 = jnp.where(tpos == 0, p2, jnp.where(tpos == 1, p1, r2))
    cw = w_ref[...]
    y = cw[0:1, :] * x2 + cw[1:2, :] * x1 + cw[2:3, :] * w3
    o_ref[...] = (cb_ref[...] * y.reshape(sb * ln, cd)).astype(o_ref.dtype)
    tail = w3[:, ln - 2:ln, :]
    carry_sc[:, 6:8, :] = tail

    @pl.when(ti == pl.num_programs(1) - 1)
    def _():
        so_ref[...] = tail


def _short_conv(z, offs, conv_w, nseq, seq_len, row0, state):
    if seq_len > 512:
        sb, ln = 1, _pick(seq_len, 512)
    else:
        ln = seq_len
        sb = _pick(nseq, max(1, 256 // ln), 1)
    nt = seq_len // ln
    rows = sb * ln
    blk0 = row0 // rows

    def zs(off):
        return pl.BlockSpec((rows, CONV_D), lambda b, t: (blk0 + b * nt + t, off // CONV_D))

    st_spec = pl.BlockSpec((sb, CONV_W - 1, CONV_D), lambda b, t: (b, 0, 0))
    in_specs = [zs(offs["cb"]), zs(offs["cc"]), zs(offs["cu"]), pl.BlockSpec(conv_w.shape, lambda b, t: (0, 0))]
    args = [z, z, z, conv_w]
    if state is not None:
        in_specs.append(st_spec)
        args.append(state)
    return pl.pallas_call(
        functools.partial(_conv_kernel, sb=sb, rows_per_seq=ln, has_state=state is not None),
        out_shape=(jax.ShapeDtypeStruct((nseq * seq_len, CONV_D), BF),
                   jax.ShapeDtypeStruct((nseq, CONV_W - 1, CONV_D), F32)),
        grid=(nseq // sb, nt),
        in_specs=in_specs,
        out_specs=(pl.BlockSpec((rows, CONV_D), lambda b, t: (b * nt + t, 0)), st_spec),
        scratch_shapes=[pltpu.VMEM((sb, 8, CONV_D), F32)],
        compiler_params=_cparams(("parallel", "arbitrary")),
        name="short_conv",
    )(*args)


def _merge_kernel(oa_ref, ob_ref, oc_ref, wa_ref, wb_ref, wc_ref, ga_ref, gb_ref, gc_ref, o_ref):
    def br(o, w, g):
        return jax.nn.sigmoid(g[...]) * jnp.dot(o[...], w[...].astype(BF), preferred_element_type=F32)

    m = br(oa_ref, wa_ref, ga_ref) + br(ob_ref, wb_ref, gb_ref) + br(oc_ref, wc_ref, gc_ref)
    o_ref[...] = m.astype(o_ref.dtype)


def _merge(oa, ob, oc, wa, wb, wc, layer, z, offs, d):
    t = oa.shape[0]
    tm = _pick(t, 1024, 16)
    tn = _pick(d, 512, LANES)
    osp = lambda a: pl.BlockSpec((tm, a.shape[1]), lambda i, j: (i, 0))
    wsp = lambda w: pl.BlockSpec((None, w.shape[1], tn), lambda i, j: (layer, 0, j))
    gsp = lambda off: pl.BlockSpec((tm, tn), lambda i, j: (i, off // tn + j))
    return pl.pallas_call(
        _merge_kernel,
        out_shape=jax.ShapeDtypeStruct((t, d), BF),
        grid=(t // tm, d // tn),
        in_specs=[osp(oa), osp(ob), osp(oc), wsp(wa), wsp(wb), wsp(wc),
                  gsp(offs["ga"]), gsp(offs["gb"]), gsp(offs["gc"])],
        out_specs=pl.BlockSpec((tm, tn), lambda i, j: (i, j)),
        compiler_params=_cparams(("parallel", "arbitrary")),
        name="branch_merge",
    )(oa, ob, oc, wa, wb, wc, z, z, z)


def _norm_router_kernel(x_ref, g_ref, whi_ref, wlo_ref, h_ref, lg_ref):
    y = _rms(x_ref[...], g_ref[...])
    h_ref[...] = y
    yhi = y.astype(BF)
    ylo = (y - yhi.astype(F32)).astype(BF)
    whi = whi_ref[...]
    lg_ref[...] = (jnp.dot(yhi, whi, preferred_element_type=F32)
                   + (jnp.dot(ylo, whi, preferred_element_type=F32)
                      + jnp.dot(yhi, wlo_ref[...], preferred_element_type=F32)))


def _norm_router(x, g, whi, wlo):
    t, d = x.shape
    tm = _pick(t, 512)
    n = whi.shape[1]
    return pl.pallas_call(
        _norm_router_kernel,
        out_shape=(jax.ShapeDtypeStruct((t, d), F32), jax.ShapeDtypeStruct((t, n), F32)),
        grid=(t // tm,),
        in_specs=[pl.BlockSpec((tm, d), lambda i: (i, 0)), pl.BlockSpec((1, d), lambda i: (0, 0)),
                  pl.BlockSpec((d, n), lambda i: (0, 0)), pl.BlockSpec((d, n), lambda i: (0, 0))],
        out_specs=(pl.BlockSpec((tm, d), lambda i: (i, 0)), pl.BlockSpec((tm, n), lambda i: (i, 0))),
        compiler_params=_cparams(("parallel",)),
        name="norm_router",
    )(x, g.reshape(1, d), whi, wlo)


def _row_copy(src_hbm, dst, sem, src_row, dst_slot, dst_row):
    return pltpu.make_async_copy(src_hbm.at[pl.ds(src_row, 1), :],
                                 dst.at[dst_slot, pl.ds(dst_row, 1), :], sem.at[dst_slot])


def _gather_rows(idx_ref, src_hbm, dst, sem, slot, n):
    def body(r, carry):
        _row_copy(src_hbm, dst, sem, idx_ref[0, r], slot, r).start()
        return carry
    lax.fori_loop(0, n, body, 0, unroll=8)


def _wait_rows(src_hbm, dst, sem, slot, n):
    def body(r, carry):
        _row_copy(src_hbm, dst, sem, 0, slot, r).wait()
        return carry
    lax.fori_loop(0, n, body, 0, unroll=8)


def _expert_kernel(be_ref, nu_ref, tok_ref, tok_next_ref, h_hbm, sw_ref, wg_ref, wu_ref, wd_ref, y_ref,
                   xbuf, sem, wg_sc, wu_sc, wd_sc):
    b = pl.program_id(0)
    nu = nu_ref[0]
    slot = b % 2
    bm = xbuf.shape[1]

    @pl.when(b == 0)
    def _():
        _gather_rows(tok_ref, h_hbm, xbuf, sem, 0, bm)

    @pl.when(b + 1 < nu)
    def _():
        _gather_rows(tok_next_ref, h_hbm, xbuf, sem, 1 - slot, bm)

    @pl.when(b < nu)
    def _():
        _wait_rows(h_hbm, xbuf, sem, slot, bm)

        @pl.when((b == 0) | (be_ref[b] != be_ref[jnp.maximum(b - 1, 0)]))
        def _():
            wg_sc[...] = wg_ref[...].astype(BF)
            wu_sc[...] = wu_ref[...].astype(BF)
            wd_sc[...] = wd_ref[...].astype(BF)

        x = xbuf[slot].astype(BF)
        a = jnp.dot(x, wg_sc[...], preferred_element_type=F32)
        u = jnp.dot(x, wu_sc[...], preferred_element_type=F32)
        hid = (a * jax.nn.sigmoid(a) * u).astype(BF)
        y_ref[...] = jnp.dot(hid, wd_sc[...], preferred_element_type=F32) * sw_ref[...]

    @pl.when(b >= nu)
    def _():
        y_ref[...] = jnp.zeros(y_ref.shape, F32)


def _experts(hn, slot_tok, slot_w, block_expert, n_used, wg, wu, wd, layer):
    n_blocks = block_expert.shape[0]
    bm = MOE_BM
    d = hn.shape[1]
    ff = wg.shape[3]
    tok3 = slot_tok.reshape(n_blocks, 1, bm)
    smem_blk = lambda f: pl.BlockSpec((None, 1, bm), f, memory_space=pltpu.SMEM)
    wmap = lambda b, be, nu: (layer, be[jnp.minimum(b, nu[0] - 1)], 0, 0)
    return pl.pallas_call(
        _expert_kernel,
        out_shape=jax.ShapeDtypeStruct((n_blocks * bm, d), F32),
        grid_spec=pltpu.PrefetchScalarGridSpec(
            num_scalar_prefetch=2,
            grid=(n_blocks,),
            in_specs=[smem_blk(lambda b, be, nu: (b, 0, 0)),
                      smem_blk(lambda b, be, nu: (jnp.minimum(b + 1, n_blocks - 1), 0, 0)),
                      pl.BlockSpec(memory_space=pl.ANY),
                      pl.BlockSpec((bm, 1), lambda b, be, nu: (b, 0)),
                      pl.BlockSpec((None, None, d, ff), wmap),
                      pl.BlockSpec((None, None, d, ff), wmap),
                      pl.BlockSpec((None, None, ff, d), wmap)],
            out_specs=pl.BlockSpec((bm, d), lambda b, be, nu: (b, 0)),
            scratch_shapes=[pltpu.VMEM((2, bm, d), F32), pltpu.SemaphoreType.DMA((2,)),
                            pltpu.VMEM((d, ff), BF), pltpu.VMEM((d, ff), BF), pltpu.VMEM((ff, d), BF)]),
        compiler_params=_cparams(("arbitrary",)),
        name="moe_experts",
    )(block_expert, n_used, tok3, tok3, hn, slot_w.reshape(-1, 1), wg, wu, wd)


def _combine_kernel(d_ref, d_next_ref, x_ref, y_hbm, o_ref, ybuf, sem):
    i = pl.program_id(0)
    slot = i % 2
    n = ybuf.shape[1]
    tm = n // TOP_K

    @pl.when(i == 0)
    def _():
        _gather_rows(d_ref, y_hbm, ybuf, sem, 0, n)

    @pl.when(i + 1 < pl.num_programs(0))
    def _():
        _gather_rows(d_next_ref, y_hbm, ybuf, sem, 1 - slot, n)

    _wait_rows(y_hbm, ybuf, sem, slot, n)
    o_ref[...] = x_ref[...] + (ybuf[slot, 0:tm, :] + ybuf[slot, tm:n, :])


def _combine(x, y, dest):
    t, d = x.shape
    tm = _pick(t, 128)
    nt = t // tm
    d3 = dest.reshape(nt, tm, TOP_K).transpose(0, 2, 1).reshape(nt, 1, TOP_K * tm)
    smem_blk = lambda f: pl.BlockSpec((None, 1, TOP_K * tm), f, memory_space=pltpu.SMEM)
    return pl.pallas_call(
        _combine_kernel,
        out_shape=jax.ShapeDtypeStruct((t, d), F32),
        grid=(nt,),
        in_specs=[smem_blk(lambda i: (i, 0, 0)),
                  smem_blk(lambda i: (jnp.minimum(i + 1, nt - 1), 0, 0)),
                  pl.BlockSpec((tm, d), lambda i: (i, 0)),
                  pl.BlockSpec(memory_space=pl.ANY)],
        out_specs=pl.BlockSpec((tm, d), lambda i: (i, 0)),
        scratch_shapes=[pltpu.VMEM((2, TOP_K * tm, d), F32), pltpu.SemaphoreType.DMA((2,))],
        compiler_params=_cparams(("arbitrary",)),
        name="moe_combine",
    )(d3, d3, x, y)


def _route(logits, bias_g, bias_e):
    t = logits.shape[0]
    g_logits = logits[:, :N_GROUPS] + bias_g
    e_logits = (logits[:, N_GROUPS:N_GROUPS + N_EXPERTS] + bias_e).reshape(t, N_GROUPS, EXPERTS_PER_GROUP)
    p_group = jax.nn.softmax(g_logits, axis=-1)
    grp = jnp.argmax(g_logits, axis=-1).astype(jnp.int32)
    in_grp = jnp.take_along_axis(e_logits, grp[:, None, None], axis=1)[:, 0]
    top_v, top_i = lax.top_k(in_grp, TOP_K)
    gate = jax.nn.softmax(top_v, axis=-1) * jnp.take_along_axis(p_group, grp[:, None], axis=1)
    expert = grp[:, None] * EXPERTS_PER_GROUP + top_i.astype(jnp.int32)

    bm = MOE_BM
    n_assign = t * TOP_K
    e_flat = expert.reshape(-1)
    onehot = (e_flat[:, None] == jnp.arange(N_EXPERTS, dtype=jnp.int32)[None, :]).astype(jnp.int32)
    rank_all = jnp.cumsum(onehot, axis=0)
    counts = rank_all[-1]
    rank = jnp.take_along_axis(rank_all, e_flat[:, None], axis=1)[:, 0] - 1
    padded = (counts + bm - 1) // bm * bm
    pad_end = jnp.cumsum(padded)
    pad_start = pad_end - padded
    dest = (pad_start[e_flat] + rank).astype(jnp.int32)
    n_blocks = (n_assign + N_EXPERTS * (bm - 1) + bm - 1) // bm
    n_slots = n_blocks * bm
    tok = jnp.repeat(jnp.arange(t, dtype=jnp.int32), TOP_K)
    slot_tok = jnp.zeros((n_slots,), jnp.int32).at[dest].set(tok)
    slot_w = jnp.zeros((n_slots,), F32).at[dest].set(gate.reshape(-1))
    block_expert = jnp.minimum(
        jnp.searchsorted(pad_end, jnp.arange(n_blocks, dtype=jnp.int32) * bm, side="right"),
        N_EXPERTS - 1).astype(jnp.int32)
    n_used = (pad_end[-1] // bm).astype(jnp.int32).reshape(1)
    return slot_tok, slot_w, block_expert, n_used, dest.reshape(t, TOP_K)


def _rope_tables(pos, half):
    inv_freq = ROPE_BASE ** (-jnp.arange(half, dtype=F32) / half)
    ang = pos.astype(F32)[:, None] * inv_freq[None, :]
    return jnp.cos(ang), jnp.sin(ang)


def _z_layout(d):
    segs = [("ga", d), ("gb", d), ("gc", d),
            ("rv", RET_HEADS * RET_DV), ("rg", RET_HEADS * RET_DV),
            ("cb", CONV_D), ("cc", CONV_D), ("cu", CONV_D),
            ("cq", Q_LORA), ("rq", RET_HEADS * RET_DK), ("rk", RET_HEADS * RET_DK),
            ("ckv", KV_LORA), ("kpe", LANES)]
    offs, o = {}, 0
    for name, w in segs:
        assert o % w == 0
        offs[name] = o
        o += w
    return segs, offs, o


IN_PROJ_TN = 256


def _in_proj_tiles(d):
    src_names = ["cq", "ckv", "kpe", "rq", "rk", "rv", "rg", "cb", "cc", "cu", "ga", "gb", "gc"]
    src_w = [Q_LORA, KV_LORA, MLA_ROPE, RET_HEADS * RET_DK, RET_HEADS * RET_DK, RET_HEADS * RET_DV,
             RET_HEADS * RET_DV, CONV_D, CONV_D, CONV_D, d, d, d]
    src_off = dict(zip(src_names, np.cumsum([0] + src_w[:-1]).tolist()))
    segs, offs, total = _z_layout(d)
    tn = IN_PROJ_TN
    tiles = []
    for name, w in segs[:-2]:
        assert w % tn == 0
        tiles += [src_off[name] + i * tn for i in range(w // tn)]
    assert [n for n, _ in segs[-2:]] == ["ckv", "kpe"] and src_off["kpe"] == src_off["ckv"] + KV_LORA
    assert KV_LORA + LANES == tn and src_off["ckv"] + tn <= sum(src_w)
    tiles.append(src_off["ckv"])
    assert len(tiles) * tn == total
    return tiles, offs


def kernel(x_prompt, x_sample, cache_ckv, cache_kpe, state_ret, state_conv, page_table, norm_mix_g, w_in, mla_g_cq, mla_g_ckv, mla_w_uq, mla_w_uk, mla_w_uv, mla_g_qn, mla_g_kn, ret_g, conv_w, w_branch_a, w_branch_b, w_branch_c, w_out, norm_ffn_g, router_group_w, router_group_b, router_expert_w, router_expert_b, expert_w_gate, expert_w_up, expert_w_down):
    nb, s, d = x_prompt.shape
    ns, dq, _ = x_sample.shape
    depth = w_in.shape[0]
    tp, ts = nb * s, ns * dq
    past_len = page_table.shape[1] * PAGE_SIZE

    x = jnp.concatenate([x_prompt.reshape(tp, d), x_sample.reshape(ts, d)], axis=0)
    pos = jnp.concatenate([jnp.tile(jnp.arange(s, dtype=jnp.int32), nb),
                           jnp.tile(past_len + jnp.arange(dq, dtype=jnp.int32), ns)])
    cm, sm = _rope_tables(pos, MLA_ROPE // 2)
    zm = jnp.zeros_like(sm)
    mla_tabs = (jnp.concatenate([cm, cm, cm, cm], axis=1),
                jnp.concatenate([-sm, zm, -sm, zm], axis=1),
                jnp.concatenate([zm, sm, zm, sm], axis=1))
    cr, sr = _rope_tables(pos, RET_DK // 2)
    ret_tabs = (jnp.concatenate([cr, cr], axis=1), jnp.concatenate([-sr, sr], axis=1))

    uq_perm = np.concatenate(
        [h * MLA_QK + np.arange(MLA_NOPE) for h in range(MLA_HEADS)]
        + [h * MLA_QK + MLA_NOPE + np.arange(MLA_ROPE) for h in range(MLA_HEADS)])

    outs = {k: [] for k in ("ckv_p", "kpe_p", "ret_p", "conv_p", "ckv_s", "kpe_s", "ret_s", "conv_s")}
    w_in_t = jnp.swapaxes(w_in, 1, 2)
    cache_kpe_t = jnp.swapaxes(cache_kpe, 2, 3)
    in_tiles, offs = _in_proj_tiles(d)
    for l in range(depth):
        wuq_p = mla_w_uq[l][:, uq_perm].astype(BF)
        wukt = mla_w_uk[l].T.astype(BF)
        wuv = mla_w_uv[l].astype(BF)
        gqn = mla_g_qn[l][:MLA_NOPE].reshape(1, -1)
        gqr = jnp.tile(mla_g_qn[l][MLA_NOPE:], 2).reshape(1, -1)
        gkn = mla_g_kn[l][:MLA_NOPE].reshape(1, -1)
        gkr = jnp.tile(mla_g_kn[l][MLA_NOPE:], 2).reshape(1, -1)

        xn = _norm_cast(x, norm_mix_g[l])
        z = _in_proj(xn, w_in_t, l, in_tiles, IN_PROJ_TN)

        c_lat, k_rot, q, kcat, rs = _mla_prep(
            z, offs, mla_tabs, mla_g_cq[l].reshape(1, -1), mla_g_ckv[l].reshape(1, -1),
            wuq_p, wukt, gqn, gqr, gkn, gkr)
        oa_p = _attn_prompt(q, kcat, rs, wuv, nb, s)
        oa_s = _attn_paged(page_table, q, c_lat[tp:].reshape(ns, dq, KV_LORA),
                           jnp.swapaxes(k_rot[tp:].reshape(ns, dq, MLA_ROPE), 1, 2), wukt, wuv,
                           cache_ckv, cache_kpe_t, l, tp // dq)
        oa = jnp.concatenate([oa_p, oa_s.reshape(ts, -1).astype(BF)], axis=0)

        ob_p, ret_p = _retention(z, offs, ret_tabs, ret_g[l], nb, s, 0, None)
        ob_s, ret_s = _retention(z, offs, ret_tabs, ret_g[l], ns, dq, tp, state_ret[l])
        ob = jnp.concatenate([ob_p, ob_s], axis=0)

        oc_p, conv_p = _short_conv(z, offs, conv_w[l], nb, s, 0, None)
        oc_s, conv_s = _short_conv(z, offs, conv_w[l], ns, dq, tp, state_conv[l])
        oc = jnp.concatenate([oc_p, oc_s], axis=0)

        m = _merge(oa, ob, oc, w_branch_a, w_branch_b, w_branch_c, l, z, offs, d)
        x = _matmul(m, w_out, l, residual=x, tm_target=1024, tn_target=512, name="out_proj")

        wr = jnp.concatenate([router_group_w[l], router_expert_w[l]], axis=1)
        wr = jnp.pad(wr, ((0, 0), (0, LANES - wr.shape[1])))
        wr_hi = wr.astype(BF)
        wr_lo = (wr - wr_hi.astype(F32)).astype(BF)
        hn, logits = _norm_router(x, norm_ffn_g[l], wr_hi, wr_lo)
        slot_tok, slot_w, block_expert, n_used, dest = _route(logits, router_group_b[l], router_expert_b[l])
        y = _experts(hn, slot_tok, slot_w, block_expert, n_used,
                     expert_w_gate, expert_w_up, expert_w_down, l)
        x = _combine(x, y, dest)

        outs["ckv_p"].append(c_lat[:tp].reshape(nb, s, KV_LORA))
        outs["kpe_p"].append(k_rot[:tp].reshape(nb, s, MLA_ROPE))
        outs["ret_p"].append(ret_p)
        outs["conv_p"].append(conv_p)
        outs["ckv_s"].append(c_lat[tp:].reshape(ns, dq, KV_LORA))
        outs["kpe_s"].append(k_rot[tp:].reshape(ns, dq, MLA_ROPE))
        outs["ret_s"].append(ret_s)
        outs["conv_s"].append(conv_s)

    st = lambda k: jnp.stack(outs[k])
    return (x[:tp].reshape(nb, s, d), x[tp:].reshape(ns, dq, d),
            st("ckv_p"), st("kpe_p"), st("ret_p"), st("conv_p"),
            st("ckv_s"), st("kpe_s"), st("ret_s"), st("conv_s"))
```

```python
import functools
import math

import numpy as np
import jax
import jax.numpy as jnp
from jax import lax
from jax.experimental import pallas as pl
from jax.experimental.pallas import tpu as pltpu

BF = jnp.bfloat16
F32 = jnp.float32

MLA_HEADS = 8
MLA_NOPE = 128
MLA_ROPE = 64
MLA_QK = MLA_NOPE + MLA_ROPE
MLA_V = 128
Q_LORA = 512
KV_LORA = 128
RET_HEADS = 4
RET_DK = 128
RET_DV = 256
RET_CHUNK = 128
CONV_D = 1024
CONV_W = 3
N_GROUPS = 8
EXPERTS_PER_GROUP = 8
N_EXPERTS = N_GROUPS * EXPERTS_PER_GROUP
TOP_K = 2
EXPERT_FF = 512
PAGE_SIZE = 128
ROPE_BASE = 10000.0
NORM_EPS = 1e-6
ATTN_SCALE = MLA_QK ** -0.5
LOG_G = tuple(math.log1p(-(2.0 ** (-5.0 - h))) for h in range(RET_HEADS))

LANES = 128
QPAD = 256
VMEM_LIMIT = 56 * 1024 * 1024
MOE_BM = 128
NEG_INF = float("-inf")

_NT = (((1,), (1,)), ((), ()))


def _cparams(sem, vmem=VMEM_LIMIT):
    return pltpu.CompilerParams(dimension_semantics=sem, vmem_limit_bytes=vmem)


def _pick(n, target, mult=8):
    if n <= target:
        return n
    for d in range(target, 0, -1):
        if n % d == 0 and d % mult == 0:
            return d
    return n


def _rms(x, g):
    return x * lax.rsqrt(jnp.mean(x * x, axis=-1, keepdims=True) + NORM_EPS) * g


def _lane_sumsq_t(x):
    sq = x * x
    hi = sq.astype(BF)
    r1 = sq - hi.astype(F32)
    mid = r1.astype(BF)
    lo = (r1 - mid.astype(F32)).astype(BF)
    ones = jnp.ones((8, x.shape[1]), BF)
    nt = lambda b: lax.dot_general(ones, b, _NT, preferred_element_type=F32)
    return nt(hi) + (nt(mid) + nt(lo))


def _norm_kernel(x_ref, g_ref, o_ref):
    o_ref[...] = _rms(x_ref[...], g_ref[...]).astype(o_ref.dtype)


def _norm_cast(x, g):
    t, d = x.shape
    tm = _pick(t, 512)
    return pl.pallas_call(
        _norm_kernel,
        out_shape=jax.ShapeDtypeStruct((t, d), BF),
        grid=(t // tm,),
        in_specs=[pl.BlockSpec((tm, d), lambda i: (i, 0)), pl.BlockSpec((1, d), lambda i: (0, 0))],
        out_specs=pl.BlockSpec((tm, d), lambda i: (i, 0)),
        compiler_params=_cparams(("parallel",)),
        name="norm_cast",
    )(x, g.reshape(1, d))


def _mm_kernel(a_ref, w_ref, o_ref):
    o_ref[...] = jnp.dot(a_ref[...].astype(BF), w_ref[...].astype(BF),
                         preferred_element_type=F32).astype(o_ref.dtype)


def _mm_res_kernel(a_ref, w_ref, r_ref, o_ref):
    o_ref[...] = r_ref[...] + jnp.dot(a_ref[...].astype(BF), w_ref[...].astype(BF),
                                      preferred_element_type=F32)


def _in_proj_kernel(offs_ref, a_ref, w_ref, o_ref):
    del offs_ref
    o_ref[...] = lax.dot_general(a_ref[...], w_ref[0].astype(BF), _NT, preferred_element_type=F32)


def _in_proj(xn, w_in_t, layer, tile_rows, tn):
    m, k = xn.shape
    tm = _pick(m, 2304, 16)
    nt = len(tile_rows)
    unit = math.gcd(*tile_rows, tn)
    assert unit % 8 == 0
    offs = jnp.asarray([r // unit for r in tile_rows], jnp.int32)
    return pl.pallas_call(
        _in_proj_kernel,
        out_shape=jax.ShapeDtypeStruct((m, nt * tn), F32),
        grid_spec=pltpu.PrefetchScalarGridSpec(
            num_scalar_prefetch=1,
            grid=(m // tm, nt),
            in_specs=[pl.BlockSpec((tm, k), lambda i, j, offs: (i, 0)),
                      pl.BlockSpec((pl.Element(1), pl.Element(tn), pl.Element(k)),
                                   lambda i, j, offs: (layer, offs[j] * unit, 0))],
            out_specs=pl.BlockSpec((tm, tn), lambda i, j, offs: (i, j))),
        compiler_params=_cparams(("parallel", "arbitrary")),
        name="in_proj",
    )(offs, xn, w_in_t)


def _matmul(a, w3, layer, residual=None, tm_target=2304, tn_target=256, name="matmul"):
    m, k = a.shape
    n = w3.shape[2]
    tm = _pick(m, tm_target, 16)
    tn = _pick(n, tn_target, LANES)
    in_specs = [pl.BlockSpec((tm, k), lambda i, j: (i, 0)),
                pl.BlockSpec((None, k, tn), lambda i, j: (layer, 0, j))]
    args = [a, w3]
    body = _mm_kernel
    if residual is not None:
        in_specs.append(pl.BlockSpec((tm, tn), lambda i, j: (i, j)))
        args.append(residual)
        body = _mm_res_kernel
    return pl.pallas_call(
        body,
        out_shape=jax.ShapeDtypeStruct((m, n), F32),
        grid=(m // tm, n // tn),
        in_specs=in_specs,
        out_specs=pl.BlockSpec((tm, tn), lambda i, j: (i, j)),
        compiler_params=_cparams(("parallel", "arbitrary")),
        name=name,
    )(*args)


def _prep_kernel(cq_ref, ckv_ref, kpe_ref, c_ref, s1_ref, s2_ref, gcq_ref, gckv_ref, wuq_ref, wukt_ref,
                 gqn_ref, gqr_ref, gkn_ref, gkr_ref,
                 clat_ref, krot_ref, q_ref, kcat_ref, rs_ref):
    tm = cq_ref.shape[0]
    cs, s1, s2 = c_ref[...], s1_ref[...], s2_ref[...]

    def rope64(x):
        return x * cs + pltpu.roll(x, 96, 1) * s1 + pltpu.roll(x, 32, 1) * s2

    low = lax.broadcasted_iota(jnp.int32, (tm, LANES), 1) < MLA_ROPE

    c_lat = _rms(ckv_ref[...], gckv_ref[...])
    clat_ref[...] = c_lat
    k_rot = jnp.where(low, rope64(kpe_ref[...]), 0.0)
    krot_ref[...] = k_rot[:, :MLA_ROPE]
    clb = c_lat.astype(BF)
    kcat_ref[:, 0:KV_LORA] = clb
    kcat_ref[:, KV_LORA:QPAD] = k_rot.astype(BF)

    knt = lax.dot_general(wukt_ref[...], clb, _NT, preferred_element_type=F32)
    ssq = jnp.sum((knt * knt).reshape(MLA_HEADS, MLA_NOPE, tm), axis=1)
    rs_ref[...] = lax.rsqrt((ssq + _lane_sumsq_t(k_rot)) / MLA_QK + NORM_EPS)

    cqn = _rms(cq_ref[...], gcq_ref[...]).astype(BF)
    q = jnp.dot(cqn, wuq_ref[...], preferred_element_type=F32)
    gqn, gqr, gkn, gkr = gqn_ref[...], gqr_ref[...], gkn_ref[...], gkr_ref[...]
    nope_w = MLA_HEADS * MLA_NOPE
    for j in range(MLA_HEADS // 2):
        xr = rope64(q[:, nope_w + LANES * j: nope_w + LANES * (j + 1)])
        sq = xr * xr
        ss = (jnp.sum(jnp.where(low, sq, 0.0), axis=-1, keepdims=True),
              jnp.sum(jnp.where(low, 0.0, sq), axis=-1, keepdims=True))
        for e in range(2):
            h = 2 * j + e
            nope = q[:, MLA_NOPE * h: MLA_NOPE * (h + 1)]
            r = lax.rsqrt((jnp.sum(nope * nope, axis=-1, keepdims=True) + ss[e]) / MLA_QK + NORM_EPS)
            qg = (nope * r * gqn * gkn).astype(BF)
            q_ref[h, :, 0:KV_LORA] = jnp.dot(qg, wukt_ref[MLA_NOPE * h: MLA_NOPE * (h + 1), :],
                                             preferred_element_type=F32)
            rp = xr * r * gqr * gkr
            if e == 1:
                rp = pltpu.roll(rp, MLA_ROPE, 1)
            q_ref[h, :, KV_LORA:QPAD] = jnp.where(low, rp, 0.0)


def _mla_prep(z, offs, tabs, gcq, gckv, wuq_p, wukt, gqn, gqr, gkn, gkr):
    t = z.shape[0]
    tm = _pick(t, 256, 16)
    cs, s1, s2 = tabs

    def zs(width, off):
        return pl.BlockSpec((tm, width), lambda i: (i, off // width))

    row = lambda w: pl.BlockSpec((tm, w), lambda i: (i, 0))
    full = lambda a: pl.BlockSpec(a.shape, lambda i: (0,) * a.ndim)
    small = [gcq, gckv, wuq_p, wukt, gqn, gqr, gkn, gkr]
    return pl.pallas_call(
        _prep_kernel,
        out_shape=(jax.ShapeDtypeStruct((t, KV_LORA), F32),
                   jax.ShapeDtypeStruct((t, MLA_ROPE), F32),
                   jax.ShapeDtypeStruct((MLA_HEADS, t, QPAD), F32),
                   jax.ShapeDtypeStruct((t, QPAD), BF),
                   jax.ShapeDtypeStruct((MLA_HEADS, t), F32)),
        grid=(t // tm,),
        in_specs=[zs(Q_LORA, offs["cq"]), zs(KV_LORA, offs["ckv"]), zs(LANES, offs["kpe"]),
                  row(LANES), row(LANES), row(LANES)] + [full(a) for a in small],
        out_specs=(row(KV_LORA), row(MLA_ROPE),
                   pl.BlockSpec((MLA_HEADS, tm, QPAD), lambda i: (0, i, 0)),
                   row(QPAD),
                   pl.BlockSpec((MLA_HEADS, tm), lambda i: (0, i))),
        compiler_params=_cparams(("parallel",)),
        name="mla_prep",
    )(z, z, z, cs, s1, s2, *small)


PROMPT_HEAD_GROUPS = 4


def _attn_prompt_kernel(q_ref, k_ref, rs_ref, wuv_ref, o_ref, qb_sc, m_sc, l_sc, acc_sc, *, bq, bk):
    qi = pl.program_id(1)
    ki = pl.program_id(2)
    rows = MLA_HEADS * bq

    @pl.when(ki == 0)
    def _():
        qb_sc[...] = q_ref[...].reshape(rows, QPAD).astype(BF)
        m_sc[...] = jnp.full(m_sc.shape, NEG_INF, F32)
        l_sc[...] = jnp.zeros(l_sc.shape, F32)
        acc_sc[...] = jnp.zeros(acc_sc.shape, F32)

    def update(masked):
        k = k_ref[...]
        rs = rs_ref[...] * ATTN_SCALE
        hg = MLA_HEADS // PROMPT_HEAD_GROUPS
        gr = hg * bq
        for g in range(PROMPT_HEAD_GROUPS):
            rsl = slice(g * gr, (g + 1) * gr)
            s = lax.dot_general(qb_sc[rsl, :], k, _NT, preferred_element_type=F32)
            s3 = s.reshape(hg, bq, bk) * rs[g * hg:(g + 1) * hg, None, :]
            if masked:
                qpos = qi * bq + lax.broadcasted_iota(jnp.int32, (1, bq, bk), 1)
                kpos = ki * bk + lax.broadcasted_iota(jnp.int32, (1, bq, bk), 2)
                s3 = jnp.where(kpos <= qpos, s3, NEG_INF)
            s2 = s3.reshape(gr, bk)
            m_prev = m_sc[rsl, :]
            m_new = jnp.maximum(m_prev, jnp.max(s2, axis=-1, keepdims=True))
            alpha = jnp.exp(m_prev - m_new)
            p = jnp.exp(s2 - m_new)
            l_sc[rsl, :] = alpha * l_sc[rsl, :] + jnp.sum(p, axis=-1, keepdims=True)
            acc_sc[rsl, :] = alpha * acc_sc[rsl, :] + jnp.dot(p.astype(BF), k[:, 0:KV_LORA],
                                                              preferred_element_type=F32)
            m_sc[rsl, :] = m_new

    first_q, last_q = qi * bq, qi * bq + (bq - 1)
    last_k = ki * bk + (bk - 1)

    @pl.when(last_k <= first_q)
    def _():
        update(False)

    @pl.when((last_k > first_q) & (ki * bk <= last_q))
    def _():
        update(True)

    @pl.when(ki == pl.num_programs(2) - 1)
    def _():
        ol = (acc_sc[...] / l_sc[...]).astype(BF)
        for h in range(MLA_HEADS):
            o_ref[:, MLA_V * h: MLA_V * (h + 1)] = jnp.dot(
                ol[h * bq:(h + 1) * bq], wuv_ref[:, MLA_V * h: MLA_V * (h + 1)],
                preferred_element_type=F32).astype(o_ref.dtype)


def _attn_prompt(q, kcat, rs, wuv, nb, s):
    bq = _pick(s, 128, 16)
    bk = _pick(s, 512, LANES)
    nq, nk = s // bq, s // bk
    rows = MLA_HEADS * bq

    def kblk(b, qi, ki):
        return b * nk + jnp.minimum(ki, (qi * bq + bq - 1) // bk)

    return pl.pallas_call(
        functools.partial(_attn_prompt_kernel, bq=bq, bk=bk),
        out_shape=jax.ShapeDtypeStruct((nb * s, MLA_HEADS * MLA_V), BF),
        grid=(nb, nq, nk),
        in_specs=[pl.BlockSpec((MLA_HEADS, bq, QPAD), lambda b, qi, ki: (0, b * nq + qi, 0)),
                  pl.BlockSpec((bk, QPAD), lambda b, qi, ki: (kblk(b, qi, ki), 0)),
                  pl.BlockSpec((MLA_HEADS, bk), lambda b, qi, ki: (0, kblk(b, qi, ki))),
                  pl.BlockSpec(wuv.shape, lambda b, qi, ki: (0, 0))],
        out_specs=pl.BlockSpec((bq, MLA_HEADS * MLA_V), lambda b, qi, ki: (b * nq + qi, 0)),
        scratch_shapes=[pltpu.VMEM((rows, QPAD), BF), pltpu.VMEM((rows, 1), F32),
                        pltpu.VMEM((rows, 1), F32), pltpu.VMEM((rows, KV_LORA), F32)],
        compiler_params=_cparams(("parallel", "parallel", "arbitrary")),
        name="attn_prompt",
    )(q, kcat, rs, wuv)


def _attn_paged_kernel(pt_ref, q_ref, cnew_ref, krnew_ref, wukt_ref, wuv_ref, ckv_hbm, kpe_hbm, o_ref,
                       wq_sc, qr_sc, m_sc, l_sc, acc_sc, cpad_sc, krpad_sc, sprev_sc, cbprev_sc,
                       cbuf, krbuf, sem, *, pp, dq, layer):
    j = pl.program_id(1)
    step = pl.program_id(0) * pl.num_programs(1) + j
    last_step = pl.num_programs(0) * pl.num_programs(1) - 1
    slot = step % 2
    rows = MLA_HEADS * dq
    nk = MLA_HEADS * MLA_NOPE

    def page_copies(src_step, dst_slot):
        cps = []
        for i in range(pp):
            page = pt_ref[src_step * pp + i]
            cps.append(pltpu.make_async_copy(ckv_hbm.at[layer, page], cbuf.at[dst_slot, i], sem.at[dst_slot]))
            cps.append(pltpu.make_async_copy(kpe_hbm.at[layer, page], krbuf.at[dst_slot, i], sem.at[dst_slot]))
        return cps

    def fetch_next_and_wait():
        for cp in page_copies(jnp.minimum(step + 1, last_step), 1 - slot):
            cp.start()
        for cp in page_copies(step, slot):
            cp.wait()

    @pl.when(step == 0)
    def _():
        for cp in page_copies(0, 0):
            cp.start()

    @pl.when(j == 0)
    def _():
        q = q_ref[...].reshape(rows, QPAD)
        wq_sc[0:nk, :] = wukt_ref[...]
        wq_sc[nk:nk + rows, :] = q[:, 0:KV_LORA].astype(BF)
        qr_sc[...] = q[:, KV_LORA:KV_LORA + MLA_ROPE].astype(BF)
        m_sc[...] = jnp.full(m_sc.shape, NEG_INF, F32)
        l_sc[...] = jnp.zeros(l_sc.shape, F32)
        acc_sc[...] = jnp.zeros(acc_sc.shape, F32)

    def scores(c, krt):
        n = c.shape[0]
        cb = c.astype(BF)
        big = lax.dot_general(wq_sc[...], cb, _NT, preferred_element_type=F32)
        knt = big[0:nk]
        ssq = jnp.sum((knt * knt).reshape(MLA_HEADS, MLA_NOPE, n), axis=1)
        krsq = jnp.sum(krt * krt, axis=0, keepdims=True)
        rs = lax.rsqrt((ssq + krsq) / MLA_QK + NORM_EPS) * ATTN_SCALE
        s = big[nk:nk + rows] + jnp.dot(qr_sc[...], krt.astype(BF), preferred_element_type=F32)
        return (s.reshape(MLA_HEADS, dq, n) * rs[:, None, :]).reshape(rows, n), cb

    def softmax_update(s, cb):
        m_prev = m_sc[...]
        m_new = jnp.maximum(m_prev, jnp.max(s, axis=-1, keepdims=True))
        alpha = jnp.exp(m_prev - m_new)
        p = jnp.exp(s - m_new)
        l_sc[...] = alpha * l_sc[...] + jnp.sum(p, axis=-1, keepdims=True)
        acc_sc[...] = alpha * acc_sc[...] + jnp.dot(p.astype(BF), cb, preferred_element_type=F32)
        m_sc[...] = m_new

    def page_chunk():
        return (jnp.concatenate([cbuf[slot, i] for i in range(pp)], axis=0),
                jnp.concatenate([krbuf[slot, i] for i in range(pp)], axis=1))

    @pl.when(j == 0)
    def _():
        fetch_next_and_wait()
        s, cb = scores(*page_chunk())
        sprev_sc[...] = s
        cbprev_sc[...] = cb

    @pl.when(j > 0)
    def _():
        fetch_next_and_wait()
        s_prev, cb_prev = sprev_sc[...], cbprev_sc[...]
        s, cb = scores(*page_chunk())
        softmax_update(s_prev, cb_prev)
        sprev_sc[...] = s
        cbprev_sc[...] = cb

    @pl.when(step == last_step)
    def _():
        for cp in page_copies(last_step, 1 - slot):
            cp.wait()

    @pl.when(j == pl.num_programs(1) - 1)
    def _():
        softmax_update(sprev_sc[...], cbprev_sc[...])
        cpad_sc[...] = jnp.zeros(cpad_sc.shape, F32)
        krpad_sc[...] = jnp.zeros(krpad_sc.shape, F32)
        cpad_sc[0:dq, :] = cnew_ref[...]
        krpad_sc[:, 0:dq] = krnew_ref[...]
        t = lax.broadcasted_iota(jnp.int32, (rows, PAGE_SIZE), 1)
        r = lax.broadcasted_iota(jnp.int32, (MLA_HEADS, dq, PAGE_SIZE), 1).reshape(rows, PAGE_SIZE)
        s, cb = scores(cpad_sc[...], krpad_sc[...])
        softmax_update(jnp.where(t <= r, s, NEG_INF), cb)
        ol = (acc_sc[...] / l_sc[...]).astype(BF)
        full = jnp.dot(ol, wuv_ref[...], preferred_element_type=F32)
        for h in range(MLA_HEADS):
            o_ref[:, MLA_V * h: MLA_V * (h + 1)] = full[h * dq:(h + 1) * dq, MLA_V * h: MLA_V * (h + 1)]


PAGES_PER_STEP = 16


def _attn_paged(page_table, q, c_new, kr_new_t, wukt, wuv, cache_ckv, cache_kpe_t, layer, row_blk0):
    nseq, n_pages = page_table.shape
    dq = c_new.shape[1]
    pp = math.gcd(n_pages, PAGES_PER_STEP)
    n_chunks = n_pages // pp
    pt = page_table.reshape(-1)

    in_specs = [pl.BlockSpec((MLA_HEADS, dq, QPAD), lambda b, j, pt_ref: (0, row_blk0 + b, 0)),
                pl.BlockSpec((None, dq, KV_LORA), lambda b, j, pt_ref: (b, 0, 0)),
                pl.BlockSpec((None, MLA_ROPE, dq), lambda b, j, pt_ref: (b, 0, 0)),
                pl.BlockSpec(wukt.shape, lambda b, j, pt_ref: (0, 0)),
                pl.BlockSpec(wuv.shape, lambda b, j, pt_ref: (0, 0)),
                pl.BlockSpec(memory_space=pl.ANY), pl.BlockSpec(memory_space=pl.ANY)]
    rows = MLA_HEADS * dq
    return pl.pallas_call(
        functools.partial(_attn_paged_kernel, pp=pp, dq=dq, layer=layer),
        out_shape=jax.ShapeDtypeStruct((nseq, dq, MLA_HEADS * MLA_V), F32),
        grid_spec=pltpu.PrefetchScalarGridSpec(
            num_scalar_prefetch=1,
            grid=(nseq, n_chunks),
            in_specs=in_specs,
            out_specs=pl.BlockSpec((None, dq, MLA_HEADS * MLA_V), lambda b, j, pt_ref: (b, 0, 0)),
            scratch_shapes=[pltpu.VMEM((MLA_HEADS * MLA_NOPE + rows, KV_LORA), BF),
                            pltpu.VMEM((rows, MLA_ROPE), BF),
                            pltpu.VMEM((rows, 1), F32), pltpu.VMEM((rows, 1), F32),
                            pltpu.VMEM((rows, KV_LORA), F32),
                            pltpu.VMEM((PAGE_SIZE, KV_LORA), F32), pltpu.VMEM((MLA_ROPE, PAGE_SIZE), F32),
                            pltpu.VMEM((rows, pp * PAGE_SIZE), F32), pltpu.VMEM((pp * PAGE_SIZE, KV_LORA), BF),
                            pltpu.VMEM((2, pp, PAGE_SIZE, KV_LORA), F32),
                            pltpu.VMEM((2, pp, MLA_ROPE, PAGE_SIZE), F32),
                            pltpu.SemaphoreType.DMA((2,))]),
        compiler_params=_cparams(("arbitrary", "arbitrary")),
        name="attn_paged",
    )(pt, q, c_new, kr_new_t, wukt, wuv, cache_ckv, cache_kpe_t)


def _ret_kernel(rq_ref, rk_ref, rv_ref, rg_ref, cf_ref, sf_ref, g_ref, *rest, chunk, sb, has_state):
    if has_state:
        st_ref, o_ref, so_ref, r_sc, kp_sc, kd_sc, vp_sc, o_sc = rest
    else:
        o_ref, so_ref, r_sc, kp_sc, kd_sc, vp_sc, o_sc = rest
    ci = pl.program_id(1)
    lp = kp_sc.shape[0]

    @pl.when(ci == 0)
    def _():
        if has_state:
            r_sc[...] = st_ref[...]
        else:
            r_sc[...] = jnp.zeros(r_sc.shape, F32)

    if chunk < lp:
        kp_sc[...] = jnp.zeros(kp_sc.shape, F32)
        kd_sc[...] = jnp.zeros(kd_sc.shape, F32)
        vp_sc[...] = jnp.zeros(vp_sc.shape, F32)

    def pad(x, sc):
        if chunk == lp:
            return x
        sc[0:chunk, :] = x
        return sc[...]

    t = lax.broadcasted_iota(jnp.int32, (chunk, 1), 0).astype(F32)
    rel = (lax.broadcasted_iota(jnp.int32, (chunk, lp), 0)
           - lax.broadcasted_iota(jnp.int32, (chunk, lp), 1)).astype(F32)
    for si in range(sb):
        rows = slice(si * chunk, (si + 1) * chunk)
        cf, sf = cf_ref[rows, :], sf_ref[rows, :]
        rope = lambda x: x * cf + pltpu.roll(x, RET_DK // 2, 1) * sf
        for h in range(RET_HEADS):
            lg = LOG_G[h]
            q = rope(rq_ref[rows, RET_DK * h: RET_DK * (h + 1)])
            k = rope(rk_ref[rows, RET_DK * h: RET_DK * (h + 1)]) * (RET_DK ** -0.5)
            v = rv_ref[rows, RET_DV * h: RET_DV * (h + 1)]
            qb = q.astype(BF)
            kp = pad(k, kp_sc).astype(BF)
            vp = pad(v, vp_sc).astype(BF)
            decay = jnp.where(rel >= 0.0, jnp.exp(lg * jnp.maximum(rel, 0.0)), 0.0)
            s = lax.dot_general(qb, kp, _NT, preferred_element_type=F32) * decay
            r_old = r_sc[si, h]
            o = jnp.dot(s.astype(BF), vp, preferred_element_type=F32)
            o = o + jnp.dot(qb, r_old.astype(BF), preferred_element_type=F32) * jnp.exp(lg * (t + 1.0))
            kd = pad(k * jnp.exp(lg * (chunk - 1.0 - t)), kd_sc)
            r_sc[si, h] = math.exp(lg * chunk) * r_old + jnp.dot(
                kd.T.astype(BF), vp, preferred_element_type=F32)
            on = _rms(o, g_ref[h:h + 1, :])
            rg = rg_ref[rows, RET_DV * h: RET_DV * (h + 1)]
            o_sc[rows, RET_DV * h: RET_DV * (h + 1)] = rg * jax.nn.sigmoid(rg) * on
    o_ref[...] = o_sc[...].astype(o_ref.dtype)

    @pl.when(ci == pl.num_programs(1) - 1)
    def _():
        so_ref[...] = r_sc[...]


def _retention(z, offs, tabs, ret_g, nseq, seq_len, row0, state, layer=0):
    chunk = RET_CHUNK if seq_len % RET_CHUNK == 0 else seq_len
    nc = seq_len // chunk
    sb = 1 if nc > 1 else _pick(nseq, max(1, 64 // chunk), 1)
    rows = sb * chunk
    lp = max(chunk, LANES)
    blk0 = row0 // rows
    cf, sf = tabs
    dk, dv = RET_HEADS * RET_DK, RET_HEADS * RET_DV

    def zs(width, off):
        return pl.BlockSpec((rows, width), lambda b, c: (blk0 + b * nc + c, off // width))

    tab = pl.BlockSpec((rows, RET_DK), lambda b, c: (blk0 + b * nc + c, 0))
    st_spec = pl.BlockSpec((sb, RET_HEADS, RET_DK, RET_DV), lambda b, c: (b, 0, 0, 0))
    in_specs = [zs(dk, offs["rq"]), zs(dk, offs["rk"]), zs(dv, offs["rv"]), zs(dv, offs["rg"]), tab, tab,
                pl.BlockSpec(ret_g.shape, lambda b, c: (0, 0))]
    args = [z, z, z, z, cf, sf, ret_g]
    if state is not None:
        in_specs.append(pl.BlockSpec((None, sb, RET_HEADS, RET_DK, RET_DV), lambda b, c: (layer, b, 0, 0, 0)))
        args.append(state)
    return pl.pallas_call(
        functools.partial(_ret_kernel, chunk=chunk, sb=sb, has_state=state is not None),
        out_shape=(jax.ShapeDtypeStruct((nseq * seq_len, dv), BF),
                   jax.ShapeDtypeStruct((nseq, RET_HEADS, RET_DK, RET_DV), F32)),
        grid=(nseq // sb, nc),
        in_specs=in_specs,
        out_specs=(pl.BlockSpec((rows, dv), lambda b, c: (b * nc + c, 0)), st_spec),
        scratch_shapes=[pltpu.VMEM((sb, RET_HEADS, RET_DK, RET_DV), F32),
                        pltpu.VMEM((lp, RET_DK), F32), pltpu.VMEM((lp, RET_DK), F32),
                        pltpu.VMEM((lp, RET_DV), F32), pltpu.VMEM((rows, dv), F32)],
        compiler_params=_cparams(("parallel", "arbitrary")),
        name="retention",
    )(*args)


def _conv_kernel(cb_ref, cc_ref, cu_ref, w_ref, *rest, sb, rows_per_seq, has_state):
    if has_state:
        buf_ref, o_ref, so_ref, carry_sc = rest
    else:
        o_ref, so_ref, carry_sc = rest
    ti = pl.program_id(1)
    ln = rows_per_seq
    cd = cb_ref.shape[1]

    @pl.when(ti == 0)
    def _():
        if has_state:
            carry_sc[:, 6:8, :] = buf_ref[...]
        else:
            carry_sc[...] = jnp.zeros(carry_sc.shape, F32)

    w2 = cc_ref[...] * cu_ref[...]
    w3 = w2.reshape(sb, ln, cd)
    r1 = pltpu.roll(w2, 1, 0).reshape(sb, ln, cd)
    r2 = pltpu.roll(w2, 2, 0).reshape(sb, ln, cd)
    prev = carry_sc[...]
    p1, p2 = prev[:, 7:8, :], prev[:, 6:7, :]
    tpos = lax.broadcasted_iota(jnp.int32, (sb, ln, cd), 1)
    x1 = jnp.where(tpos == 0, p1, r1)
    x2 = jnp.where(tpos == 0, p2, jnp.where(tpos == 1, p1, r2))
    cw = w_ref[...]
    y = cw[0:1, :] * x2 + cw[1:2, :] * x1 + cw[2:3, :] * w3
    o_ref[...] = (cb_ref[...] * y.reshape(sb * ln, cd)).astype(o_ref.dtype)
    tail = w3[:, ln - 2:ln, :]
    carry_sc[:, 6:8, :] = tail

    @pl.when(ti == pl.num_programs(1) - 1)
    def _():
        so_ref[...] = tail


def _short_conv(z, offs, conv_w, nseq, seq_len, row0, state):
    if seq_len > 512:
        sb, ln = 1, _pick(seq_len, 512)
    else:
        ln = seq_len
        sb = _pick(nseq, max(1, 256 // ln), 1)
    nt = seq_len // ln
    rows = sb * ln
    blk0 = row0 // rows

    def zs(off):
        return pl.BlockSpec((rows, CONV_D), lambda b, t: (blk0 + b * nt + t, off // CONV_D))

    st_spec = pl.BlockSpec((sb, CONV_W - 1, CONV_D), lambda b, t: (b, 0, 0))
    in_specs = [zs(offs["cb"]), zs(offs["cc"]), zs(offs["cu"]), pl.BlockSpec(conv_w.shape, lambda b, t: (0, 0))]
    args = [z, z, z, conv_w]
    if state is not None:
        in_specs.append(st_spec)
        args.append(state)
    return pl.pallas_call(
        functools.partial(_conv_kernel, sb=sb, rows_per_seq=ln, has_state=state is not None),
        out_shape=(jax.ShapeDtypeStruct((nseq * seq_len, CONV_D), BF),
                   jax.ShapeDtypeStruct((nseq, CONV_W - 1, CONV_D), F32)),
        grid=(nseq // sb, nt),
        in_specs=in_specs,
        out_specs=(pl.BlockSpec((rows, CONV_D), lambda b, t: (b * nt + t, 0)), st_spec),
        scratch_shapes=[pltpu.VMEM((sb, 8, CONV_D), F32)],
        compiler_params=_cparams(("parallel", "arbitrary")),
        name="short_conv",
    )(*args)


def _merge_kernel(oa_ref, ob_ref, oc_ref, wa_ref, wb_ref, wc_ref, ga_ref, gb_ref, gc_ref, o_ref):
    def br(o, w, g):
        return jax.nn.sigmoid(g[...]) * jnp.dot(o[...], w[...].astype(BF), preferred_element_type=F32)

    m = br(oa_ref, wa_ref, ga_ref) + br(ob_ref, wb_ref, gb_ref) + br(oc_ref, wc_ref, gc_ref)
    o_ref[...] = m.astype(o_ref.dtype)


def _merge(oa, ob, oc, wa, wb, wc, layer, z, offs, d):
    t = oa.shape[0]
    tm = _pick(t, 1024, 16)
    tn = _pick(d, 512, LANES)
    osp = lambda a: pl.BlockSpec((tm, a.shape[1]), lambda i, j: (i, 0))
    wsp = lambda w: pl.BlockSpec((None, w.shape[1], tn), lambda i, j: (layer, 0, j))
    gsp = lambda off: pl.BlockSpec((tm, tn), lambda i, j: (i, off // tn + j))
    return pl.pallas_call(
        _merge_kernel,
        out_shape=jax.ShapeDtypeStruct((t, d), BF),
        grid=(t // tm, d // tn),
        in_specs=[osp(oa), osp(ob), osp(oc), wsp(wa), wsp(wb), wsp(wc),
                  gsp(offs["ga"]), gsp(offs["gb"]), gsp(offs["gc"])],
        out_specs=pl.BlockSpec((tm, tn), lambda i, j: (i, j)),
        compiler_params=_cparams(("parallel", "arbitrary")),
        name="branch_merge",
    )(oa, ob, oc, wa, wb, wc, z, z, z)


def _norm_router_kernel(x_ref, g_ref, whi_ref, wlo_ref, h_ref, lg_ref):
    y = _rms(x_ref[...], g_ref[...])
    h_ref[...] = y
    yhi = y.astype(BF)
    ylo = (y - yhi.astype(F32)).astype(BF)
    whi = whi_ref[...]
    lg_ref[...] = (jnp.dot(yhi, whi, preferred_element_type=F32)
                   + (jnp.dot(ylo, whi, preferred_element_type=F32)
                      + jnp.dot(yhi, wlo_ref[...], preferred_element_type=F32)))


def _norm_router(x, g, whi, wlo):
    t, d = x.shape
    tm = _pick(t, 512)
    n = whi.shape[1]
    return pl.pallas_call(
        _norm_router_kernel,
        out_shape=(jax.ShapeDtypeStruct((t, d), F32), jax.ShapeDtypeStruct((t, n), F32)),
        grid=(t // tm,),
        in_specs=[pl.BlockSpec((tm, d), lambda i: (i, 0)), pl.BlockSpec((1, d), lambda i: (0, 0)),
                  pl.BlockSpec((d, n), lambda i: (0, 0)), pl.BlockSpec((d, n), lambda i: (0, 0))],
        out_specs=(pl.BlockSpec((tm, d), lambda i: (i, 0)), pl.BlockSpec((tm, n), lambda i: (i, 0))),
        compiler_params=_cparams(("parallel",)),
        name="norm_router",
    )(x, g.reshape(1, d), whi, wlo)


def _row_copy(src_hbm, dst, sem, src_row, dst_slot, dst_row):
    return pltpu.make_async_copy(src_hbm.at[pl.ds(src_row, 1), :],
                                 dst.at[dst_slot, pl.ds(dst_row, 1), :], sem.at[dst_slot])


def _gather_rows(idx_ref, src_hbm, dst, sem, slot, n):
    def body(r, carry):
        _row_copy(src_hbm, dst, sem, idx_ref[0, r], slot, r).start()
        return carry
    lax.fori_loop(0, n, body, 0, unroll=8)


def _wait_rows(src_hbm, dst, sem, slot, n):
    def body(r, carry):
        _row_copy(src_hbm, dst, sem, 0, slot, r).wait()
        return carry
    lax.fori_loop(0, n, body, 0, unroll=8)


def _expert_kernel(be_ref, nu_ref, tok_ref, tok_next_ref, h_hbm, sw_ref, wg_ref, wu_ref, wd_ref, y_ref,
                   xbuf, sem, wg_sc, wu_sc, wd_sc):
    b = pl.program_id(0)
    nu = nu_ref[0]
    slot = b % 2
    bm = xbuf.shape[1]

    @pl.when(b == 0)
    def _():
        _gather_rows(tok_ref, h_hbm, xbuf, sem, 0, bm)

    @pl.when(b < nu)
    def _():
        _wait_rows(h_hbm, xbuf, sem, slot, bm)

        @pl.when((b == 0) | (be_ref[b] != be_ref[jnp.maximum(b - 1, 0)]))
        def _():
            wg_sc[...] = wg_ref[...].astype(BF)
            wu_sc[...] = wu_ref[...].astype(BF)
            wd_sc[...] = wd_ref[...].astype(BF)

        for r in range(bm):
            _row_copy(h_hbm, xbuf, sem, tok_next_ref[0, r], 1 - slot, r).start()

        x = xbuf[slot].astype(BF)
        a = jnp.dot(x, wg_sc[...], preferred_element_type=F32)
        u = jnp.dot(x, wu_sc[...], preferred_element_type=F32)
        hid = (a * jax.nn.sigmoid(a) * u).astype(BF)
        y_ref[...] = jnp.dot(hid, wd_sc[...], preferred_element_type=F32) * sw_ref[...]

    @pl.when(b == nu - 1)
    def _():
        _wait_rows(h_hbm, xbuf, sem, 1 - slot, bm)

    @pl.when(b >= nu)
    def _():
        y_ref[...] = jnp.zeros(y_ref.shape, F32)


def _experts(hn, slot_tok, slot_w, block_expert, n_used, wg, wu, wd, layer):
    n_blocks = block_expert.shape[0]
    bm = MOE_BM
    d = hn.shape[1]
    ff = wg.shape[3]
    tok3 = slot_tok.reshape(n_blocks, 1, bm)
    smem_blk = lambda f: pl.BlockSpec((None, 1, bm), f, memory_space=pltpu.SMEM)
    wmap = lambda b, be, nu: (layer, be[jnp.minimum(b, nu[0] - 1)], 0, 0)
    return pl.pallas_call(
        _expert_kernel,
        out_shape=jax.ShapeDtypeStruct((n_blocks * bm, d), F32),
        grid_spec=pltpu.PrefetchScalarGridSpec(
            num_scalar_prefetch=2,
            grid=(n_blocks,),
            in_specs=[smem_blk(lambda b, be, nu: (b, 0, 0)),
                      smem_blk(lambda b, be, nu: (jnp.minimum(b + 1, n_blocks - 1), 0, 0)),
                      pl.BlockSpec(memory_space=pl.ANY),
                      pl.BlockSpec((bm, 1), lambda b, be, nu: (b, 0)),
                      pl.BlockSpec((None, None, d, ff), wmap),
                      pl.BlockSpec((None, None, d, ff), wmap),
                      pl.BlockSpec((None, None, ff, d), wmap)],
            out_specs=pl.BlockSpec((bm, d), lambda b, be, nu: (b, 0)),
            scratch_shapes=[pltpu.VMEM((2, bm, d), F32), pltpu.SemaphoreType.DMA((2,)),
                            pltpu.VMEM((d, ff), BF), pltpu.VMEM((d, ff), BF), pltpu.VMEM((ff, d), BF)]),
        compiler_params=_cparams(("arbitrary",)),
        name="moe_experts",
    )(block_expert, n_used, tok3, tok3, hn, slot_w.reshape(-1, 1), wg, wu, wd)


def _combine_kernel(d_ref, d_next_ref, x_ref, y_hbm, o_ref, ybuf, sem):
    i = pl.program_id(0)
    slot = i % 2
    n = ybuf.shape[1]
    tm = n // TOP_K

    @pl.when(i == 0)
    def _():
        _gather_rows(d_ref, y_hbm, ybuf, sem, 0, n)

    @pl.when(i + 1 < pl.num_programs(0))
    def _():
        _gather_rows(d_next_ref, y_hbm, ybuf, sem, 1 - slot, n)

    _wait_rows(y_hbm, ybuf, sem, slot, n)
    o_ref[...] = x_ref[...] + (ybuf[slot, 0:tm, :] + ybuf[slot, tm:n, :])


def _combine(x, y, dest):
    t, d = x.shape
    tm = _pick(t, 128)
    nt = t // tm
    d3 = dest.reshape(nt, tm, TOP_K).transpose(0, 2, 1).reshape(nt, 1, TOP_K * tm)
    smem_blk = lambda f: pl.BlockSpec((None, 1, TOP_K * tm), f, memory_space=pltpu.SMEM)
    return pl.pallas_call(
        _combine_kernel,
        out_shape=jax.ShapeDtypeStruct((t, d), F32),
        grid=(nt,),
        in_specs=[smem_blk(lambda i: (i, 0, 0)),
                  smem_blk(lambda i: (jnp.minimum(i + 1, nt - 1), 0, 0)),
                  pl.BlockSpec((tm, d), lambda i: (i, 0)),
                  pl.BlockSpec(memory_space=pl.ANY)],
        out_specs=pl.BlockSpec((tm, d), lambda i: (i, 0)),
        scratch_shapes=[pltpu.VMEM((2, TOP_K * tm, d), F32), pltpu.SemaphoreType.DMA((2,))],
        compiler_params=_cparams(("arbitrary",)),
        name="moe_combine",
    )(d3, d3, x, y)


def _route(logits, bias_g, bias_e):
    t = logits.shape[0]
    g_logits = logits[:, :N_GROUPS] + bias_g
    e_logits = (logits[:, N_GROUPS:N_GROUPS + N_EXPERTS] + bias_e).reshape(t, N_GROUPS, EXPERTS_PER_GROUP)
    p_group = jax.nn.softmax(g_logits, axis=-1)
    grp = jnp.argmax(g_logits, axis=-1).astype(jnp.int32)
    in_grp = jnp.take_along_axis(e_logits, grp[:, None, None], axis=1)[:, 0]
    top_v, top_i = lax.top_k(in_grp, TOP_K)
    gate = jax.nn.softmax(top_v, axis=-1) * jnp.take_along_axis(p_group, grp[:, None], axis=1)
    expert = grp[:, None] * EXPERTS_PER_GROUP + top_i.astype(jnp.int32)

    bm = MOE_BM
    n_assign = t * TOP_K
    e_flat = expert.reshape(-1)
    onehot = (e_flat[:, None] == jnp.arange(N_EXPERTS, dtype=jnp.int32)[None, :]).astype(jnp.int32)
    rank_all = jnp.cumsum(onehot, axis=0)
    counts = rank_all[-1]
    rank = jnp.take_along_axis(rank_all, e_flat[:, None], axis=1)[:, 0] - 1
    padded = (counts + bm - 1) // bm * bm
    pad_end = jnp.cumsum(padded)
    pad_start = pad_end - padded
    dest = (pad_start[e_flat] + rank).astype(jnp.int32)
    n_blocks = (n_assign + N_EXPERTS * (bm - 1) + bm - 1) // bm
    n_slots = n_blocks * bm
    tok = jnp.repeat(jnp.arange(t, dtype=jnp.int32), TOP_K)
    slot_tok = jnp.zeros((n_slots,), jnp.int32).at[dest].set(tok)
    slot_w = jnp.zeros((n_slots,), F32).at[dest].set(gate.reshape(-1))
    block_expert = jnp.minimum(
        jnp.searchsorted(pad_end, jnp.arange(n_blocks, dtype=jnp.int32) * bm, side="right"),
        N_EXPERTS - 1).astype(jnp.int32)
    n_used = (pad_end[-1] // bm).astype(jnp.int32).reshape(1)
    return slot_tok, slot_w, block_expert, n_used, dest.reshape(t, TOP_K)


def _rope_tables(pos, half):
    inv_freq = ROPE_BASE ** (-jnp.arange(half, dtype=F32) / half)
    ang = pos.astype(F32)[:, None] * inv_freq[None, :]
    return jnp.cos(ang), jnp.sin(ang)


def _z_layout(d):
    segs = [("ga", d), ("gb", d), ("gc", d),
            ("rv", RET_HEADS * RET_DV), ("rg", RET_HEADS * RET_DV),
            ("cb", CONV_D), ("cc", CONV_D), ("cu", CONV_D),
            ("cq", Q_LORA), ("rq", RET_HEADS * RET_DK), ("rk", RET_HEADS * RET_DK),
            ("ckv", KV_LORA), ("kpe", LANES)]
    offs, o = {}, 0
    for name, w in segs:
        assert o % w == 0
        offs[name] = o
        o += w
    return segs, offs, o


IN_PROJ_TN = 256


def _in_proj_tiles(d):
    src_names = ["cq", "ckv", "kpe", "rq", "rk", "rv", "rg", "cb", "cc", "cu", "ga", "gb", "gc"]
    src_w = [Q_LORA, KV_LORA, MLA_ROPE, RET_HEADS * RET_DK, RET_HEADS * RET_DK, RET_HEADS * RET_DV,
             RET_HEADS * RET_DV, CONV_D, CONV_D, CONV_D, d, d, d]
    src_off = dict(zip(src_names, np.cumsum([0] + src_w[:-1]).tolist()))
    segs, offs, total = _z_layout(d)
    tn = IN_PROJ_TN
    tiles = []
    for name, w in segs[:-2]:
        assert w % tn == 0
        tiles += [src_off[name] + i * tn for i in range(w // tn)]
    assert [n for n, _ in segs[-2:]] == ["ckv", "kpe"] and src_off["kpe"] == src_off["ckv"] + KV_LORA
    assert KV_LORA + LANES == tn and src_off["ckv"] + tn <= sum(src_w)
    tiles.append(src_off["ckv"])
    assert len(tiles) * tn == total
    return tiles, offs


def kernel(x_prompt, x_sample, cache_ckv, cache_kpe, state_ret, state_conv, page_table, norm_mix_g, w_in, mla_g_cq, mla_g_ckv, mla_w_uq, mla_w_uk, mla_w_uv, mla_g_qn, mla_g_kn, ret_g, conv_w, w_branch_a, w_branch_b, w_branch_c, w_out, norm_ffn_g, router_group_w, router_group_b, router_expert_w, router_expert_b, expert_w_gate, expert_w_up, expert_w_down):
    nb, s, d = x_prompt.shape
    ns, dq, _ = x_sample.shape
    depth = w_in.shape[0]
    tp, ts = nb * s, ns * dq
    past_len = page_table.shape[1] * PAGE_SIZE

    x = jnp.concatenate([x_prompt.reshape(tp, d), x_sample.reshape(ts, d)], axis=0)
    pos = jnp.concatenate([jnp.tile(jnp.arange(s, dtype=jnp.int32), nb),
                           jnp.tile(past_len + jnp.arange(dq, dtype=jnp.int32), ns)])
    cm, sm = _rope_tables(pos, MLA_ROPE // 2)
    zm = jnp.zeros_like(sm)
    mla_tabs = (jnp.concatenate([cm, cm, cm, cm], axis=1),
                jnp.concatenate([-sm, zm, -sm, zm], axis=1),
                jnp.concatenate([zm, sm, zm, sm], axis=1))
    cr, sr = _rope_tables(pos, RET_DK // 2)
    ret_tabs = (jnp.concatenate([cr, cr], axis=1), jnp.concatenate([-sr, sr], axis=1))

    uq_perm = np.concatenate(
        [h * MLA_QK + np.arange(MLA_NOPE) for h in range(MLA_HEADS)]
        + [h * MLA_QK + MLA_NOPE + np.arange(MLA_ROPE) for h in range(MLA_HEADS)])

    outs = {k: [] for k in ("ckv_p", "kpe_p", "ret_p", "conv_p", "ckv_s", "kpe_s", "ret_s", "conv_s")}
    w_in_t = jnp.swapaxes(w_in, 1, 2)
    cache_kpe_t = jnp.swapaxes(cache_kpe, 2, 3)
    in_tiles, offs = _in_proj_tiles(d)
    for l in range(depth):
        wuq_p = mla_w_uq[l][:, uq_perm].astype(BF)
        wukt = mla_w_uk[l].T.astype(BF)
        wuv = mla_w_uv[l].astype(BF)
        gqn = mla_g_qn[l][:MLA_NOPE].reshape(1, -1)
        gqr = jnp.tile(mla_g_qn[l][MLA_NOPE:], 2).reshape(1, -1)
        gkn = mla_g_kn[l][:MLA_NOPE].reshape(1, -1)
        gkr = jnp.tile(mla_g_kn[l][MLA_NOPE:], 2).reshape(1, -1)

        xn = _norm_cast(x, norm_mix_g[l])
        z = _in_proj(xn, w_in_t, l, in_tiles, IN_PROJ_TN)

        c_lat, k_rot, q, kcat, rs = _mla_prep(
            z, offs, mla_tabs, mla_g_cq[l].reshape(1, -1), mla_g_ckv[l].reshape(1, -1),
            wuq_p, wukt, gqn, gqr, gkn, gkr)
        oa_p = _attn_prompt(q, kcat, rs, wuv, nb, s)
        oa_s = _attn_paged(page_table, q, c_lat[tp:].reshape(ns, dq, KV_LORA),
                           jnp.swapaxes(k_rot[tp:].reshape(ns, dq, MLA_ROPE), 1, 2), wukt, wuv,
                           cache_ckv, cache_kpe_t, l, tp // dq)
        oa = jnp.concatenate([oa_p, oa_s.reshape(ts, -1).astype(BF)], axis=0)

        ob_p, ret_p = _retention(z, offs, ret_tabs, ret_g[l], nb, s, 0, None)
        ob_s, ret_s = _retention(z, offs, ret_tabs, ret_g[l], ns, dq, tp, state_ret, l)
        ob = jnp.concatenate([ob_p, ob_s], axis=0)

        oc_p, conv_p = _short_conv(z, offs, conv_w[l], nb, s, 0, None)
        oc_s, conv_s = _short_conv(z, offs, conv_w[l], ns, dq, tp, state_conv[l])
        oc = jnp.concatenate([oc_p, oc_s], axis=0)

        m = _merge(oa, ob, oc, w_branch_a, w_branch_b, w_branch_c, l, z, offs, d)
        x = _matmul(m, w_out, l, residual=x, tm_target=1024, tn_target=512, name="out_proj")

        wr = jnp.concatenate([router_group_w[l], router_expert_w[l]], axis=1)
        wr = jnp.pad(wr, ((0, 0), (0, LANES - wr.shape[1])))
        wr_hi = wr.astype(BF)
        wr_lo = (wr - wr_hi.astype(F32)).astype(BF)
        hn, logits = _norm_router(x, norm_ffn_g[l], wr_hi, wr_lo)
        slot_tok, slot_w, block_expert, n_used, dest = _route(logits, router_group_b[l], router_expert_b[l])
        y = _experts(hn, slot_tok, slot_w, block_expert, n_used,
                     expert_w_gate, expert_w_up, expert_w_down, l)
        x = _combine(x, y, dest)

        outs["ckv_p"].append(c_lat[:tp].reshape(nb, s, KV_LORA))
        outs["kpe_p"].append(k_rot[:tp].reshape(nb, s, MLA_ROPE))
        outs["ret_p"].append(ret_p)
        outs["conv_p"].append(conv_p)
        outs["ckv_s"].append(c_lat[tp:].reshape(ns, dq, KV_LORA))
        outs["kpe_s"].append(k_rot[tp:].reshape(ns, dq, MLA_ROPE))
        outs["ret_s"].append(ret_s)
        outs["conv_s"].append(conv_s)

    st = lambda k: jnp.stack(outs[k])
    return (x[:tp].reshape(nb, s, d), x[tp:].reshape(ns, dq, d),
            st("ckv_p"), st("kpe_p"), st("ret_p"), st("conv_p"),
            st("ckv_s"), st("kpe_s"), st("ret_s"), st("conv_s"))
```

```python
import functools
import math

import numpy as np
import jax
import jax.numpy as jnp
from jax import lax
from jax.experimental import pallas as pl
from jax.experimental.pallas import tpu as pltpu

BF = jnp.bfloat16
F32 = jnp.float32

MLA_HEADS = 8
MLA_NOPE = 128
MLA_ROPE = 64
MLA_QK = MLA_NOPE + MLA_ROPE
MLA_V = 128
Q_LORA = 512
KV_LORA = 128
RET_HEADS = 4
RET_DK = 128
RET_DV = 256
RET_CHUNK = 128
CONV_D = 1024
CONV_W = 3
N_GROUPS = 8
EXPERTS_PER_GROUP = 8
N_EXPERTS = N_GROUPS * EXPERTS_PER_GROUP
TOP_K = 2
EXPERT_FF = 512
PAGE_SIZE = 128
ROPE_BASE = 10000.0
NORM_EPS = 1e-6
ATTN_SCALE = MLA_QK ** -0.5
LOG_G = tuple(math.log1p(-(2.0 ** (-5.0 - h))) for h in range(RET_HEADS))

LANES = 128
QPAD = 256
VMEM_LIMIT = 56 * 1024 * 1024
MOE_BM = 256
NEG_INF = float("-inf")

_NT = (((1,), (1,)), ((), ()))


def _cparams(sem, vmem=VMEM_LIMIT):
    return pltpu.CompilerParams(dimension_semantics=sem, vmem_limit_bytes=vmem)


def _pick(n, target, mult=8):
    if n <= target:
        return n
    for d in range(target, 0, -1):
        if n % d == 0 and d % mult == 0:
            return d
    return n


def _rms(x, g):
    return x * lax.rsqrt(jnp.mean(x * x, axis=-1, keepdims=True) + NORM_EPS) * g


def _lane_sumsq_t(x):
    sq = x * x
    hi = sq.astype(BF)
    r1 = sq - hi.astype(F32)
    mid = r1.astype(BF)
    lo = (r1 - mid.astype(F32)).astype(BF)
    ones = jnp.ones((8, x.shape[1]), BF)
    nt = lambda b: lax.dot_general(ones, b, _NT, preferred_element_type=F32)
    return nt(hi) + (nt(mid) + nt(lo))


def _norm_kernel(x_ref, g_ref, o_ref):
    o_ref[...] = _rms(x_ref[...], g_ref[...]).astype(o_ref.dtype)


def _norm_cast(x, g):
    t, d = x.shape
    tm = _pick(t, 512)
    return pl.pallas_call(
        _norm_kernel,
        out_shape=jax.ShapeDtypeStruct((t, d), BF),
        grid=(t // tm,),
        in_specs=[pl.BlockSpec((tm, d), lambda i: (i, 0)), pl.BlockSpec((1, d), lambda i: (0, 0))],
        out_specs=pl.BlockSpec((tm, d), lambda i: (i, 0)),
        compiler_params=_cparams(("parallel",)),
        name="norm_cast",
    )(x, g.reshape(1, d))


def _mm_kernel(a_ref, w_ref, o_ref):
    o_ref[...] = jnp.dot(a_ref[...].astype(BF), w_ref[...].astype(BF),
                         preferred_element_type=F32).astype(o_ref.dtype)


def _mm_res_kernel(a_ref, w_ref, r_ref, o_ref):
    o_ref[...] = r_ref[...] + jnp.dot(a_ref[...].astype(BF), w_ref[...].astype(BF),
                                      preferred_element_type=F32)


def _in_proj_kernel(offs_ref, a_ref, w_ref, o_ref):
    del offs_ref
    o_ref[...] = lax.dot_general(a_ref[...], w_ref[0].astype(BF), _NT, preferred_element_type=F32)


def _in_proj(xn, w_in_t, layer, tile_rows, tn):
    m, k = xn.shape
    tm = _pick(m, 2304, 16)
    nt = len(tile_rows)
    unit = math.gcd(*tile_rows, tn)
    assert unit % 8 == 0
    offs = jnp.asarray([r // unit for r in tile_rows], jnp.int32)
    return pl.pallas_call(
        _in_proj_kernel,
        out_shape=jax.ShapeDtypeStruct((m, nt * tn), F32),
        grid_spec=pltpu.PrefetchScalarGridSpec(
            num_scalar_prefetch=1,
            grid=(m // tm, nt),
            in_specs=[pl.BlockSpec((tm, k), lambda i, j, offs: (i, 0)),
                      pl.BlockSpec((pl.Element(1), pl.Element(tn), pl.Element(k)),
                                   lambda i, j, offs: (layer, offs[j] * unit, 0))],
            out_specs=pl.BlockSpec((tm, tn), lambda i, j, offs: (i, j))),
        compiler_params=_cparams(("parallel", "arbitrary")),
        name="in_proj",
    )(offs, xn, w_in_t)


def _matmul(a, w3, layer, residual=None, tm_target=2304, tn_target=256, name="matmul"):
    m, k = a.shape
    n = w3.shape[2]
    tm = _pick(m, tm_target, 16)
    tn = _pick(n, tn_target, LANES)
    in_specs = [pl.BlockSpec((tm, k), lambda i, j: (i, 0)),
                pl.BlockSpec((None, k, tn), lambda i, j: (layer, 0, j))]
    args = [a, w3]
    body = _mm_kernel
    if residual is not None:
        in_specs.append(pl.BlockSpec((tm, tn), lambda i, j: (i, j)))
        args.append(residual)
        body = _mm_res_kernel
    return pl.pallas_call(
        body,
        out_shape=jax.ShapeDtypeStruct((m, n), F32),
        grid=(m // tm, n // tn),
        in_specs=in_specs,
        out_specs=pl.BlockSpec((tm, tn), lambda i, j: (i, j)),
        compiler_params=_cparams(("parallel", "arbitrary")),
        name=name,
    )(*args)


def _prep_kernel(cq_ref, ckv_ref, kpe_ref, c_ref, s1_ref, s2_ref, gcq_ref, gckv_ref, wuq_ref, wukt_ref,
                 gqn_ref, gqr_ref, gkn_ref, gkr_ref,
                 clat_ref, krot_ref, q_ref, kcat_ref, rs_ref):
    tm = cq_ref.shape[0]
    cs, s1, s2 = c_ref[...], s1_ref[...], s2_ref[...]

    def rope64(x):
        return x * cs + pltpu.roll(x, 96, 1) * s1 + pltpu.roll(x, 32, 1) * s2

    low = lax.broadcasted_iota(jnp.int32, (tm, LANES), 1) < MLA_ROPE

    c_lat = _rms(ckv_ref[...], gckv_ref[...])
    clat_ref[...] = c_lat
    k_rot = jnp.where(low, rope64(kpe_ref[...]), 0.0)
    krot_ref[...] = k_rot[:, :MLA_ROPE]
    clb = c_lat.astype(BF)
    kcat_ref[:, 0:KV_LORA] = clb
    kcat_ref[:, KV_LORA:QPAD] = k_rot.astype(BF)

    knt = lax.dot_general(wukt_ref[...], clb, _NT, preferred_element_type=F32)
    ssq = jnp.sum((knt * knt).reshape(MLA_HEADS, MLA_NOPE, tm), axis=1)
    rs_ref[...] = lax.rsqrt((ssq + _lane_sumsq_t(k_rot)) / MLA_QK + NORM_EPS)

    cqn = _rms(cq_ref[...], gcq_ref[...]).astype(BF)
    q = jnp.dot(cqn, wuq_ref[...], preferred_element_type=F32)
    gqn, gqr, gkn, gkr = gqn_ref[...], gqr_ref[...], gkn_ref[...], gkr_ref[...]
    nope_w = MLA_HEADS * MLA_NOPE
    for j in range(MLA_HEADS // 2):
        xr = rope64(q[:, nope_w + LANES * j: nope_w + LANES * (j + 1)])
        sq = xr * xr
        ss = (jnp.sum(jnp.where(low, sq, 0.0), axis=-1, keepdims=True),
              jnp.sum(jnp.where(low, 0.0, sq), axis=-1, keepdims=True))
        for e in range(2):
            h = 2 * j + e
            nope = q[:, MLA_NOPE * h: MLA_NOPE * (h + 1)]
            r = lax.rsqrt((jnp.sum(nope * nope, axis=-1, keepdims=True) + ss[e]) / MLA_QK + NORM_EPS)
            qg = (nope * r * gqn * gkn).astype(BF)
            q_ref[h, :, 0:KV_LORA] = jnp.dot(qg, wukt_ref[MLA_NOPE * h: MLA_NOPE * (h + 1), :],
                                             preferred_element_type=F32)
            rp = xr * r * gqr * gkr
            if e == 1:
                rp = pltpu.roll(rp, MLA_ROPE, 1)
            q_ref[h, :, KV_LORA:QPAD] = jnp.where(low, rp, 0.0)


def _mla_prep(z, offs, tabs, gcq, gckv, wuq_p, wukt, gqn, gqr, gkn, gkr):
    t = z.shape[0]
    tm = _pick(t, 256, 16)
    cs, s1, s2 = tabs

    def zs(width, off):
        return pl.BlockSpec((tm, width), lambda i: (i, off // width))

    row = lambda w: pl.BlockSpec((tm, w), lambda i: (i, 0))
    full = lambda a: pl.BlockSpec(a.shape, lambda i: (0,) * a.ndim)
    small = [gcq, gckv, wuq_p, wukt, gqn, gqr, gkn, gkr]
    return pl.pallas_call(
        _prep_kernel,
        out_shape=(jax.ShapeDtypeStruct((t, KV_LORA), F32),
                   jax.ShapeDtypeStruct((t, MLA_ROPE), F32),
                   jax.ShapeDtypeStruct((MLA_HEADS, t, QPAD), F32),
                   jax.ShapeDtypeStruct((t, QPAD), BF),
                   jax.ShapeDtypeStruct((MLA_HEADS, t), F32)),
        grid=(t // tm,),
        in_specs=[zs(Q_LORA, offs["cq"]), zs(KV_LORA, offs["ckv"]), zs(LANES, offs["kpe"]),
                  row(LANES), row(LANES), row(LANES)] + [full(a) for a in small],
        out_specs=(row(KV_LORA), row(MLA_ROPE),
                   pl.BlockSpec((MLA_HEADS, tm, QPAD), lambda i: (0, i, 0)),
                   row(QPAD),
                   pl.BlockSpec((MLA_HEADS, tm), lambda i: (0, i))),
        compiler_params=_cparams(("parallel",)),
        name="mla_prep",
    )(z, z, z, cs, s1, s2, *small)


PROMPT_HEAD_GROUPS = 4


def _attn_prompt_kernel(q_ref, k_ref, rs_ref, wuv_ref, o_ref, qb_sc, m_sc, l_sc, acc_sc, *, bq, bk):
    qi = pl.program_id(1)
    ki = pl.program_id(2)
    rows = MLA_HEADS * bq

    @pl.when(ki == 0)
    def _():
        qb_sc[...] = q_ref[...].reshape(rows, QPAD).astype(BF)
        m_sc[...] = jnp.full(m_sc.shape, NEG_INF, F32)
        l_sc[...] = jnp.zeros(l_sc.shape, F32)
        acc_sc[...] = jnp.zeros(acc_sc.shape, F32)

    def update(masked):
        k = k_ref[...]
        rs = rs_ref[...] * ATTN_SCALE
        hg = MLA_HEADS // PROMPT_HEAD_GROUPS
        gr = hg * bq
        for g in range(PROMPT_HEAD_GROUPS):
            rsl = slice(g * gr, (g + 1) * gr)
            s = lax.dot_general(qb_sc[rsl, :], k, _NT, preferred_element_type=F32)
            s3 = s.reshape(hg, bq, bk) * rs[g * hg:(g + 1) * hg, None, :]
            if masked:
                qpos = qi * bq + lax.broadcasted_iota(jnp.int32, (1, bq, bk), 1)
                kpos = ki * bk + lax.broadcasted_iota(jnp.int32, (1, bq, bk), 2)
                s3 = jnp.where(kpos <= qpos, s3, NEG_INF)
            s2 = s3.reshape(gr, bk)
            m_prev = m_sc[rsl, :]
            m_new = jnp.maximum(m_prev, jnp.max(s2, axis=-1, keepdims=True))
            alpha = jnp.exp(m_prev - m_new)
            p = jnp.exp(s2 - m_new)
            l_sc[rsl, :] = alpha * l_sc[rsl, :] + jnp.sum(p, axis=-1, keepdims=True)
            acc_sc[rsl, :] = alpha * acc_sc[rsl, :] + jnp.dot(p.astype(BF), k[:, 0:KV_LORA],
                                                              preferred_element_type=F32)
            m_sc[rsl, :] = m_new

    first_q, last_q = qi * bq, qi * bq + (bq - 1)
    last_k = ki * bk + (bk - 1)

    @pl.when(last_k <= first_q)
    def _():
        update(False)

    @pl.when((last_k > first_q) & (ki * bk <= last_q))
    def _():
        update(True)

    @pl.when(ki == pl.num_programs(2) - 1)
    def _():
        ol = (acc_sc[...] / l_sc[...]).astype(BF)
        for h in range(MLA_HEADS):
            o_ref[:, MLA_V * h: MLA_V * (h + 1)] = jnp.dot(
                ol[h * bq:(h + 1) * bq], wuv_ref[:, MLA_V * h: MLA_V * (h + 1)],
                preferred_element_type=F32).astype(o_ref.dtype)


def _attn_prompt(q, kcat, rs, wuv, nb, s):
    bq = _pick(s, 128, 16)
    bk = _pick(s, 512, LANES)
    nq, nk = s // bq, s // bk
    rows = MLA_HEADS * bq

    def kblk(b, qi, ki):
        return b * nk + jnp.minimum(ki, (qi * bq + bq - 1) // bk)

    return pl.pallas_call(
        functools.partial(_attn_prompt_kernel, bq=bq, bk=bk),
        out_shape=jax.ShapeDtypeStruct((nb * s, MLA_HEADS * MLA_V), BF),
        grid=(nb, nq, nk),
        in_specs=[pl.BlockSpec((MLA_HEADS, bq, QPAD), lambda b, qi, ki: (0, b * nq + qi, 0)),
                  pl.BlockSpec((bk, QPAD), lambda b, qi, ki: (kblk(b, qi, ki), 0)),
                  pl.BlockSpec((MLA_HEADS, bk), lambda b, qi, ki: (0, kblk(b, qi, ki))),
                  pl.BlockSpec(wuv.shape, lambda b, qi, ki: (0, 0))],
        out_specs=pl.BlockSpec((bq, MLA_HEADS * MLA_V), lambda b, qi, ki: (b * nq + qi, 0)),
        scratch_shapes=[pltpu.VMEM((rows, QPAD), BF), pltpu.VMEM((rows, 1), F32),
                        pltpu.VMEM((rows, 1), F32), pltpu.VMEM((rows, KV_LORA), F32)],
        compiler_params=_cparams(("parallel", "parallel", "arbitrary")),
        name="attn_prompt",
    )(q, kcat, rs, wuv)


def _attn_paged_kernel(pt_ref, q_ref, cnew_ref, krnew_ref, wukt_ref, wuv_ref, ckv_hbm, kpe_hbm, o_ref,
                       wq_sc, qr_sc, m_sc, l_sc, acc_sc, cpad_sc, krpad_sc, sprev_sc, cbprev_sc,
                       cbuf, krbuf, sem, *, pp, dq, layer):
    j = pl.program_id(1)
    step = pl.program_id(0) * pl.num_programs(1) + j
    last_step = pl.num_programs(0) * pl.num_programs(1) - 1
    slot = step % 2
    rows = MLA_HEADS * dq
    nk = MLA_HEADS * MLA_NOPE

    def page_copies(src_step, dst_slot):
        cps = []
        for i in range(pp):
            page = pt_ref[src_step * pp + i]
            cps.append(pltpu.make_async_copy(ckv_hbm.at[layer, page], cbuf.at[dst_slot, i], sem.at[dst_slot]))
            cps.append(pltpu.make_async_copy(kpe_hbm.at[layer, page], krbuf.at[dst_slot, i], sem.at[dst_slot]))
        return cps

    def fetch_next_and_wait():
        for cp in page_copies(jnp.minimum(step + 1, last_step), 1 - slot):
            cp.start()
        for cp in page_copies(step, slot):
            cp.wait()

    @pl.when(step == 0)
    def _():
        for cp in page_copies(0, 0):
            cp.start()

    @pl.when(j == 0)
    def _():
        q = q_ref[...].reshape(rows, QPAD)
        wq_sc[0:nk, :] = wukt_ref[...]
        wq_sc[nk:nk + rows, :] = q[:, 0:KV_LORA].astype(BF)
        qr_sc[...] = q[:, KV_LORA:KV_LORA + MLA_ROPE].astype(BF)
        m_sc[...] = jnp.full(m_sc.shape, NEG_INF, F32)
        l_sc[...] = jnp.zeros(l_sc.shape, F32)
        acc_sc[...] = jnp.zeros(acc_sc.shape, F32)

    def scores(c, krt):
        n = c.shape[0]
        cb = c.astype(BF)
        big = lax.dot_general(wq_sc[...], cb, _NT, preferred_element_type=F32)
        knt = big[0:nk]
        ssq = jnp.sum((knt * knt).reshape(MLA_HEADS, MLA_NOPE, n), axis=1)
        krsq = jnp.sum(krt * krt, axis=0, keepdims=True)
        rs = lax.rsqrt((ssq + krsq) / MLA_QK + NORM_EPS) * ATTN_SCALE
        s = big[nk:nk + rows] + jnp.dot(qr_sc[...], krt.astype(BF), preferred_element_type=F32)
        return (s.reshape(MLA_HEADS, dq, n) * rs[:, None, :]).reshape(rows, n), cb

    def softmax_update(s, cb):
        m_prev = m_sc[...]
        m_new = jnp.maximum(m_prev, jnp.max(s, axis=-1, keepdims=True))
        alpha = jnp.exp(m_prev - m_new)
        p = jnp.exp(s - m_new)
        l_sc[...] = alpha * l_sc[...] + jnp.sum(p, axis=-1, keepdims=True)
        acc_sc[...] = alpha * acc_sc[...] + jnp.dot(p.astype(BF), cb, preferred_element_type=F32)
        m_sc[...] = m_new

    def page_chunk():
        return (jnp.concatenate([cbuf[slot, i] for i in range(pp)], axis=0),
                jnp.concatenate([krbuf[slot, i] for i in range(pp)], axis=1))

    @pl.when(j == 0)
    def _():
        fetch_next_and_wait()
        s, cb = scores(*page_chunk())
        sprev_sc[...] = s
        cbprev_sc[...] = cb

    @pl.when(j > 0)
    def _():
        fetch_next_and_wait()
        s_prev, cb_prev = sprev_sc[...], cbprev_sc[...]
        s, cb = scores(*page_chunk())
        softmax_update(s_prev, cb_prev)
        sprev_sc[...] = s
        cbprev_sc[...] = cb

    @pl.when(step == last_step)
    def _():
        for cp in page_copies(last_step, 1 - slot):
            cp.wait()

    @pl.when(j == pl.num_programs(1) - 1)
    def _():
        softmax_update(sprev_sc[...], cbprev_sc[...])
        cpad_sc[...] = jnp.zeros(cpad_sc.shape, F32)
        krpad_sc[...] = jnp.zeros(krpad_sc.shape, F32)
        cpad_sc[0:dq, :] = cnew_ref[...]
        krpad_sc[:, 0:dq] = krnew_ref[...]
        t = lax.broadcasted_iota(jnp.int32, (rows, PAGE_SIZE), 1)
        r = lax.broadcasted_iota(jnp.int32, (MLA_HEADS, dq, PAGE_SIZE), 1).reshape(rows, PAGE_SIZE)
        s, cb = scores(cpad_sc[...], krpad_sc[...])
        softmax_update(jnp.where(t <= r, s, NEG_INF), cb)
        ol = (acc_sc[...] / l_sc[...]).astype(BF)
        full = jnp.dot(ol, wuv_ref[...], preferred_element_type=F32)
        for h in range(MLA_HEADS):
            o_ref[:, MLA_V * h: MLA_V * (h + 1)] = full[h * dq:(h + 1) * dq, MLA_V * h: MLA_V * (h + 1)]


PAGES_PER_STEP = 16


def _attn_paged(page_table, q, c_new, kr_new_t, wukt, wuv, cache_ckv, cache_kpe_t, layer, row_blk0):
    nseq, n_pages = page_table.shape
    dq = c_new.shape[1]
    pp = math.gcd(n_pages, PAGES_PER_STEP)
    n_chunks = n_pages // pp
    pt = page_table.reshape(-1)

    in_specs = [pl.BlockSpec((MLA_HEADS, dq, QPAD), lambda b, j, pt_ref: (0, row_blk0 + b, 0)),
                pl.BlockSpec((None, dq, KV_LORA), lambda b, j, pt_ref: (b, 0, 0)),
                pl.BlockSpec((None, MLA_ROPE, dq), lambda b, j, pt_ref: (b, 0, 0)),
                pl.BlockSpec(wukt.shape, lambda b, j, pt_ref: (0, 0)),
                pl.BlockSpec(wuv.shape, lambda b, j, pt_ref: (0, 0)),
                pl.BlockSpec(memory_space=pl.ANY), pl.BlockSpec(memory_space=pl.ANY)]
    rows = MLA_HEADS * dq
    return pl.pallas_call(
        functools.partial(_attn_paged_kernel, pp=pp, dq=dq, layer=layer),
        out_shape=jax.ShapeDtypeStruct((nseq, dq, MLA_HEADS * MLA_V), F32),
        grid_spec=pltpu.PrefetchScalarGridSpec(
            num_scalar_prefetch=1,
            grid=(nseq, n_chunks),
            in_specs=in_specs,
            out_specs=pl.BlockSpec((None, dq, MLA_HEADS * MLA_V), lambda b, j, pt_ref: (b, 0, 0)),
            scratch_shapes=[pltpu.VMEM((MLA_HEADS * MLA_NOPE + rows, KV_LORA), BF),
                            pltpu.VMEM((rows, MLA_ROPE), BF),
                            pltpu.VMEM((rows, 1), F32), pltpu.VMEM((rows, 1), F32),
                            pltpu.VMEM((rows, KV_LORA), F32),
                            pltpu.VMEM((PAGE_SIZE, KV_LORA), F32), pltpu.VMEM((MLA_ROPE, PAGE_SIZE), F32),
                            pltpu.VMEM((rows, pp * PAGE_SIZE), F32), pltpu.VMEM((pp * PAGE_SIZE, KV_LORA), BF),
                            pltpu.VMEM((2, pp, PAGE_SIZE, KV_LORA), F32),
                            pltpu.VMEM((2, pp, MLA_ROPE, PAGE_SIZE), F32),
                            pltpu.SemaphoreType.DMA((2,))]),
        compiler_params=_cparams(("arbitrary", "arbitrary")),
        name="attn_paged",
    )(pt, q, c_new, kr_new_t, wukt, wuv, cache_ckv, cache_kpe_t)


def _ret_kernel(rq_ref, rk_ref, rv_ref, rg_ref, cf_ref, sf_ref, g_ref, *rest, chunk, sb, has_state):
    if has_state:
        st_ref, o_ref, so_ref, r_sc, kp_sc, kd_sc, vp_sc, o_sc = rest
    else:
        o_ref, so_ref, r_sc, kp_sc, kd_sc, vp_sc, o_sc = rest
    ci = pl.program_id(1)
    lp = kp_sc.shape[0]

    @pl.when(ci == 0)
    def _():
        if has_state:
            r_sc[...] = st_ref[...]
        else:
            r_sc[...] = jnp.zeros(r_sc.shape, F32)

    if chunk < lp:
        kp_sc[...] = jnp.zeros(kp_sc.shape, F32)
        kd_sc[...] = jnp.zeros(kd_sc.shape, F32)
        vp_sc[...] = jnp.zeros(vp_sc.shape, F32)

    def pad(x, sc):
        if chunk == lp:
            return x
        sc[0:chunk, :] = x
        return sc[...]

    t = lax.broadcasted_iota(jnp.int32, (chunk, 1), 0).astype(F32)
    rel = (lax.broadcasted_iota(jnp.int32, (chunk, lp), 0)
           - lax.broadcasted_iota(jnp.int32, (chunk, lp), 1)).astype(F32)
    for si in range(sb):
        rows = slice(si * chunk, (si + 1) * chunk)
        cf, sf = cf_ref[rows, :], sf_ref[rows, :]
        rope = lambda x: x * cf + pltpu.roll(x, RET_DK // 2, 1) * sf
        for h in range(RET_HEADS):
            lg = LOG_G[h]
            q = rope(rq_ref[rows, RET_DK * h: RET_DK * (h + 1)])
            k = rope(rk_ref[rows, RET_DK * h: RET_DK * (h + 1)]) * (RET_DK ** -0.5)
            v = rv_ref[rows, RET_DV * h: RET_DV * (h + 1)]
            qb = q.astype(BF)
            kp = pad(k, kp_sc).astype(BF)
            vp = pad(v, vp_sc).astype(BF)
            decay = jnp.where(rel >= 0.0, jnp.exp(lg * jnp.maximum(rel, 0.0)), 0.0)
            s = lax.dot_general(qb, kp, _NT, preferred_element_type=F32) * decay
            r_old = r_sc[si, h]
            o = jnp.dot(s.astype(BF), vp, preferred_element_type=F32)
            o = o + jnp.dot(qb, r_old.astype(BF), preferred_element_type=F32) * jnp.exp(lg * (t + 1.0))
            kd = pad(k * jnp.exp(lg * (chunk - 1.0 - t)), kd_sc)
            r_sc[si, h] = math.exp(lg * chunk) * r_old + jnp.dot(
                kd.T.astype(BF), vp, preferred_element_type=F32)
            on = _rms(o, g_ref[h:h + 1, :])
            rg = rg_ref[rows, RET_DV * h: RET_DV * (h + 1)]
            o_sc[rows, RET_DV * h: RET_DV * (h + 1)] = rg * jax.nn.sigmoid(rg) * on
    o_ref[...] = o_sc[...].astype(o_ref.dtype)

    @pl.when(ci == pl.num_programs(1) - 1)
    def _():
        so_ref[...] = r_sc[...]


def _retention(z, offs, tabs, ret_g, nseq, seq_len, row0, state, layer=0):
    chunk = RET_CHUNK if seq_len % RET_CHUNK == 0 else seq_len
    nc = seq_len // chunk
    sb = 1 if nc > 1 else _pick(nseq, max(1, 64 // chunk), 1)
    rows = sb * chunk
    lp = max(chunk, LANES)
    blk0 = row0 // rows
    cf, sf = tabs
    dk, dv = RET_HEADS * RET_DK, RET_HEADS * RET_DV

    def zs(width, off):
        return pl.BlockSpec((rows, width), lambda b, c: (blk0 + b * nc + c, off // width))

    tab = pl.BlockSpec((rows, RET_DK), lambda b, c: (blk0 + b * nc + c, 0))
    st_spec = pl.BlockSpec((sb, RET_HEADS, RET_DK, RET_DV), lambda b, c: (b, 0, 0, 0))
    in_specs = [zs(dk, offs["rq"]), zs(dk, offs["rk"]), zs(dv, offs["rv"]), zs(dv, offs["rg"]), tab, tab,
                pl.BlockSpec(ret_g.shape, lambda b, c: (0, 0))]
    args = [z, z, z, z, cf, sf, ret_g]
    if state is not None:
        in_specs.append(pl.BlockSpec((None, sb, RET_HEADS, RET_DK, RET_DV), lambda b, c: (layer, b, 0, 0, 0)))
        args.append(state)
    return pl.pallas_call(
        functools.partial(_ret_kernel, chunk=chunk, sb=sb, has_state=state is not None),
        out_shape=(jax.ShapeDtypeStruct((nseq * seq_len, dv), BF),
                   jax.ShapeDtypeStruct((nseq, RET_HEADS, RET_DK, RET_DV), F32)),
        grid=(nseq // sb, nc),
        in_specs=in_specs,
        out_specs=(pl.BlockSpec((rows, dv), lambda b, c: (b * nc + c, 0)), st_spec),
        scratch_shapes=[pltpu.VMEM((sb, RET_HEADS, RET_DK, RET_DV), F32),
                        pltpu.VMEM((lp, RET_DK), F32), pltpu.VMEM((lp, RET_DK), F32),
                        pltpu.VMEM((lp, RET_DV), F32), pltpu.VMEM((rows, dv), F32)],
        compiler_params=_cparams(("parallel", "arbitrary")),
        name="retention",
    )(*args)


def _conv_kernel(cb_ref, cc_ref, cu_ref, w_ref, *rest, sb, rows_per_seq, has_state):
    if has_state:
        buf_ref, o_ref, so_ref, carry_sc = rest
    else:
        o_ref, so_ref, carry_sc = rest
    ti = pl.program_id(1)
    ln = rows_per_seq
    cd = cb_ref.shape[1]

    @pl.when(ti == 0)
    def _():
        if has_state:
            carry_sc[:, 6:8, :] = buf_ref[...]
        else:
            carry_sc[...] = jnp.zeros(carry_sc.shape, F32)

    w2 = cc_ref[...] * cu_ref[...]
    w3 = w2.reshape(sb, ln, cd)
    r1 = pltpu.roll(w2, 1, 0).reshape(sb, ln, cd)
    r2 = pltpu.roll(w2, 2, 0).reshape(sb, ln, cd)
    prev = carry_sc[...]
    p1, p2 = prev[:, 7:8, :], prev[:, 6:7, :]
    tpos = lax.broadcasted_iota(jnp.int32, (sb, ln, cd), 1)
    x1 = jnp.where(tpos == 0, p1, r1)
    x2 = jnp.where(tpos == 0, p2, jnp.where(tpos == 1, p1, r2))
    cw = w_ref[...]
    y = cw[0:1, :] * x2 + cw[1:2, :] * x1 + cw[2:3, :] * w3
    o_ref[...] = (cb_ref[...] * y.reshape(sb * ln, cd)).astype(o_ref.dtype)
    tail = w3[:, ln - 2:ln, :]
    carry_sc[:, 6:8, :] = tail

    @pl.when(ti == pl.num_programs(1) - 1)
    def _():
        so_ref[...] = tail


def _short_conv(z, offs, conv_w, nseq, seq_len, row0, state):
    if seq_len > 512:
        sb, ln = 1, _pick(seq_len, 512)
    else:
        ln = seq_len
        sb = _pick(nseq, max(1, 256 // ln), 1)
    nt = seq_len // ln
    rows = sb * ln
    blk0 = row0 // rows

    def zs(off):
        return pl.BlockSpec((rows, CONV_D), lambda b, t: (blk0 + b * nt + t, off // CONV_D))

    st_spec = pl.BlockSpec((sb, CONV_W - 1, CONV_D), lambda b, t: (b, 0, 0))
    in_specs = [zs(offs["cb"]), zs(offs["cc"]), zs(offs["cu"]), pl.BlockSpec(conv_w.shape, lambda b, t: (0, 0))]
    args = [z, z, z, conv_w]
    if state is not None:
        in_specs.append(st_spec)
        args.append(state)
    return pl.pallas_call(
        functools.partial(_conv_kernel, sb=sb, rows_per_seq=ln, has_state=state is not None),
        out_shape=(jax.ShapeDtypeStruct((nseq * seq_len, CONV_D), BF),
                   jax.ShapeDtypeStruct((nseq, CONV_W - 1, CONV_D), F32)),
        grid=(nseq // sb, nt),
        in_specs=in_specs,
        out_specs=(pl.BlockSpec((rows, CONV_D), lambda b, t: (b * nt + t, 0)), st_spec),
        scratch_shapes=[pltpu.VMEM((sb, 8, CONV_D), F32)],
        compiler_params=_cparams(("parallel", "arbitrary")),
        name="short_conv",
    )(*args)


def _merge_kernel(oa_ref, ob_ref, oc_ref, wa_ref, wb_ref, wc_ref, ga_ref, gb_ref, gc_ref, o_ref):
    def br(o, w, g):
        return jax.nn.sigmoid(g[...]) * jnp.dot(o[...], w[...].astype(BF), preferred_element_type=F32)

    m = br(oa_ref, wa_ref, ga_ref) + br(ob_ref, wb_ref, gb_ref) + br(oc_ref, wc_ref, gc_ref)
    o_ref[...] = m.astype(o_ref.dtype)


def _merge(oa, ob, oc, wa, wb, wc, layer, z, offs, d):
    t = oa.shape[0]
    tm = _pick(t, 1024, 16)
    tn = _pick(d, 512, LANES)
    osp = lambda a: pl.BlockSpec((tm, a.shape[1]), lambda i, j: (i, 0))
    wsp = lambda w: pl.BlockSpec((None, w.shape[1], tn), lambda i, j: (layer, 0, j))
    gsp = lambda off: pl.BlockSpec((tm, tn), lambda i, j: (i, off // tn + j))
    return pl.pallas_call(
        _merge_kernel,
        out_shape=jax.ShapeDtypeStruct((t, d), BF),
        grid=(t // tm, d // tn),
        in_specs=[osp(oa), osp(ob), osp(oc), wsp(wa), wsp(wb), wsp(wc),
                  gsp(offs["ga"]), gsp(offs["gb"]), gsp(offs["gc"])],
        out_specs=pl.BlockSpec((tm, tn), lambda i, j: (i, j)),
        compiler_params=_cparams(("parallel", "arbitrary")),
        name="branch_merge",
    )(oa, ob, oc, wa, wb, wc, z, z, z)


def _norm_router_kernel(x_ref, g_ref, whi_ref, wlo_ref, h_ref, lg_ref):
    y = _rms(x_ref[...], g_ref[...])
    h_ref[...] = y
    yhi = y.astype(BF)
    ylo = (y - yhi.astype(F32)).astype(BF)
    whi = whi_ref[...]
    lg_ref[...] = (jnp.dot(yhi, whi, preferred_element_type=F32)
                   + (jnp.dot(ylo, whi, preferred_element_type=F32)
                      + jnp.dot(yhi, wlo_ref[...], preferred_element_type=F32)))


def _norm_router(x, g, whi, wlo):
    t, d = x.shape
    tm = _pick(t, 512)
    n = whi.shape[1]
    return pl.pallas_call(
        _norm_router_kernel,
        out_shape=(jax.ShapeDtypeStruct((t, d), F32), jax.ShapeDtypeStruct((t, n), F32)),
        grid=(t // tm,),
        in_specs=[pl.BlockSpec((tm, d), lambda i: (i, 0)), pl.BlockSpec((1, d), lambda i: (0, 0)),
                  pl.BlockSpec((d, n), lambda i: (0, 0)), pl.BlockSpec((d, n), lambda i: (0, 0))],
        out_specs=(pl.BlockSpec((tm, d), lambda i: (i, 0)), pl.BlockSpec((tm, n), lambda i: (i, 0))),
        compiler_params=_cparams(("parallel",)),
        name="norm_router",
    )(x, g.reshape(1, d), whi, wlo)


def _row_copy(src_hbm, dst, sem, src_row, dst_slot, dst_row):
    return pltpu.make_async_copy(src_hbm.at[pl.ds(src_row, 1), :],
                                 dst.at[dst_slot, pl.ds(dst_row, 1), :], sem.at[dst_slot])


def _gather_rows(idx_ref, src_hbm, dst, sem, slot, n):
    def body(r, carry):
        _row_copy(src_hbm, dst, sem, idx_ref[0, r], slot, r).start()
        return carry
    lax.fori_loop(0, n, body, 0, unroll=8)


def _wait_rows(src_hbm, dst, sem, slot, n):
    def body(r, carry):
        _row_copy(src_hbm, dst, sem, 0, slot, r).wait()
        return carry
    lax.fori_loop(0, n, body, 0, unroll=8)


def _expert_kernel(be_ref, nu_ref, chg_ref, ws_ref, nx_ref, tok_ref, tok_next_ref, h_hbm,
                   wg_hbm, wu_hbm, wd_hbm, y_ref,
                   xbuf, sem, wg_sc, wu_sc, wd_sc, wg_st, wu_st, wd_st, wsem, *, layer):
    b = pl.program_id(0)
    nu = nu_ref[0]
    slot = b % 2
    bm = xbuf.shape[1]

    def weight_copies(e, ws):
        return [pltpu.make_async_copy(wg_hbm.at[layer, e], wg_st.at[ws], wsem.at[ws]),
                pltpu.make_async_copy(wu_hbm.at[layer, e], wu_st.at[ws], wsem.at[ws]),
                pltpu.make_async_copy(wd_hbm.at[layer, e], wd_st.at[ws], wsem.at[ws])]

    @pl.when(b == 0)
    def _():
        _gather_rows(tok_ref, h_hbm, xbuf, sem, 0, bm)
        for cp in weight_copies(be_ref[0], 0):
            cp.start()

    @pl.when(b < nu)
    def _():
        _wait_rows(h_hbm, xbuf, sem, slot, bm)

        @pl.when(chg_ref[b] == 1)
        def _():
            ws = ws_ref[b]
            for cp in weight_copies(be_ref[b], ws):
                cp.wait()
            wg_sc[...] = wg_st[ws].astype(BF)
            wu_sc[...] = wu_st[ws].astype(BF)
            wd_sc[...] = wd_st[ws].astype(BF)

            @pl.when(nx_ref[b] >= 0)
            def _():
                for cp in weight_copies(nx_ref[b], 1 - ws):
                    cp.start()

        for r in range(bm):
            _row_copy(h_hbm, xbuf, sem, tok_next_ref[0, r], 1 - slot, r).start()

        x = xbuf[slot].astype(BF)
        a = jnp.dot(x, wg_sc[...], preferred_element_type=F32)
        u = jnp.dot(x, wu_sc[...], preferred_element_type=F32)
        hid = (a * jax.nn.sigmoid(a) * u).astype(BF)
        y_ref[...] = jnp.dot(hid, wd_sc[...], preferred_element_type=F32)

    @pl.when(b == nu - 1)
    def _():
        _wait_rows(h_hbm, xbuf, sem, 1 - slot, bm)

    @pl.when(b >= nu)
    def _():
        y_ref[...] = jnp.zeros(y_ref.shape, F32)


def _experts(hn, slot_tok, block_expert, n_used, wg, wu, wd, layer):
    n_blocks = block_expert.shape[0]
    bm = MOE_BM
    d = hn.shape[1]
    ff = wg.shape[3]
    tok3 = slot_tok.reshape(n_blocks, 1, bm)
    smem_blk = lambda f: pl.BlockSpec((None, 1, bm), f, memory_space=pltpu.SMEM)
    blk = jnp.arange(n_blocks, dtype=jnp.int32)
    prev = jnp.concatenate([jnp.full((1,), -1, jnp.int32), block_expert[:-1]])
    changed = (block_expert != prev) & (blk < n_used[0])
    wslot = ((jnp.cumsum(changed.astype(jnp.int32)) - 1) % 2).astype(jnp.int32)
    pos = jnp.where(changed, blk, n_blocks)
    nxt_pos = lax.cummin(jnp.concatenate([pos[1:], jnp.full((1,), n_blocks, jnp.int32)]), reverse=True)
    nxt = jnp.where(nxt_pos < n_blocks, block_expert[jnp.minimum(nxt_pos, n_blocks - 1)], -1).astype(jnp.int32)
    return pl.pallas_call(
        functools.partial(_expert_kernel, layer=layer),
        out_shape=jax.ShapeDtypeStruct((n_blocks * bm, d), F32),
        grid_spec=pltpu.PrefetchScalarGridSpec(
            num_scalar_prefetch=5,
            grid=(n_blocks,),
            in_specs=[smem_blk(lambda b, *_: (b, 0, 0)),
                      smem_blk(lambda b, *_: (jnp.minimum(b + 1, n_blocks - 1), 0, 0)),
                      pl.BlockSpec(memory_space=pl.ANY),
                      pl.BlockSpec(memory_space=pl.ANY), pl.BlockSpec(memory_space=pl.ANY),
                      pl.BlockSpec(memory_space=pl.ANY)],
            out_specs=pl.BlockSpec((bm, d), lambda b, *_: (b, 0)),
            scratch_shapes=[pltpu.VMEM((2, bm, d), F32), pltpu.SemaphoreType.DMA((2,)),
                            pltpu.VMEM((d, ff), BF), pltpu.VMEM((d, ff), BF), pltpu.VMEM((ff, d), BF),
                            pltpu.VMEM((2, d, ff), F32), pltpu.VMEM((2, d, ff), F32),
                            pltpu.VMEM((2, ff, d), F32), pltpu.SemaphoreType.DMA((2,))]),
        compiler_params=_cparams(("arbitrary",)),
        name="moe_experts",
    )(block_expert, n_used, changed.astype(jnp.int32), wslot, nxt, tok3, tok3, hn, wg, wu, wd)


def _combine_kernel(d_ref, d_next_ref, x_ref, g_ref, y_hbm, o_ref, ybuf, sem):
    i = pl.program_id(0)
    slot = i % 2
    n = ybuf.shape[1]
    tm = n // TOP_K

    @pl.when(i == 0)
    def _():
        _gather_rows(d_ref, y_hbm, ybuf, sem, 0, n)

    @pl.when(i + 1 < pl.num_programs(0))
    def _():
        _gather_rows(d_next_ref, y_hbm, ybuf, sem, 1 - slot, n)

    _wait_rows(y_hbm, ybuf, sem, slot, n)
    g = g_ref[...]
    o_ref[...] = x_ref[...] + (ybuf[slot, 0:tm, :] * g[:, 0:1] + ybuf[slot, tm:n, :] * g[:, 1:2])


def _combine(x, y, dest, gate):
    t, d = x.shape
    tm = _pick(t, 128)
    nt = t // tm
    d3 = dest.reshape(nt, tm, TOP_K).transpose(0, 2, 1).reshape(nt, 1, TOP_K * tm)
    smem_blk = lambda f: pl.BlockSpec((None, 1, TOP_K * tm), f, memory_space=pltpu.SMEM)
    return pl.pallas_call(
        _combine_kernel,
        out_shape=jax.ShapeDtypeStruct((t, d), F32),
        grid=(nt,),
        in_specs=[smem_blk(lambda i: (i, 0, 0)),
                  smem_blk(lambda i: (jnp.minimum(i + 1, nt - 1), 0, 0)),
                  pl.BlockSpec((tm, d), lambda i: (i, 0)),
                  pl.BlockSpec((tm, TOP_K), lambda i: (i, 0)),
                  pl.BlockSpec(memory_space=pl.ANY)],
        out_specs=pl.BlockSpec((tm, d), lambda i: (i, 0)),
        scratch_shapes=[pltpu.VMEM((2, TOP_K * tm, d), F32), pltpu.SemaphoreType.DMA((2,))],
        compiler_params=_cparams(("arbitrary",)),
        name="moe_combine",
    )(d3, d3, x, gate, y)


def _route(logits, bias_g, bias_e):
    t = logits.shape[0]
    g_logits = logits[:, :N_GROUPS] + bias_g
    e_logits = (logits[:, N_GROUPS:N_GROUPS + N_EXPERTS] + bias_e).reshape(t, N_GROUPS, EXPERTS_PER_GROUP)
    p_group = jax.nn.softmax(g_logits, axis=-1)
    grp = jnp.argmax(g_logits, axis=-1).astype(jnp.int32)
    in_grp = jnp.take_along_axis(e_logits, grp[:, None, None], axis=1)[:, 0]
    top_v, top_i = lax.top_k(in_grp, TOP_K)
    gate = jax.nn.softmax(top_v, axis=-1) * jnp.take_along_axis(p_group, grp[:, None], axis=1)
    expert = grp[:, None] * EXPERTS_PER_GROUP + top_i.astype(jnp.int32)

    bm = MOE_BM
    n_assign = t * TOP_K
    e_flat = expert.reshape(-1)
    onehot = (e_flat[:, None] == jnp.arange(N_EXPERTS, dtype=jnp.int32)[None, :]).astype(jnp.int32)
    rank_all = jnp.cumsum(onehot, axis=0)
    counts = rank_all[-1]
    rank = jnp.take_along_axis(rank_all, e_flat[:, None], axis=1)[:, 0] - 1
    padded = (counts + bm - 1) // bm * bm
    pad_end = jnp.cumsum(padded)
    pad_start = pad_end - padded
    dest = (pad_start[e_flat] + rank).astype(jnp.int32)
    n_blocks = (n_assign + N_EXPERTS * (bm - 1) + bm - 1) // bm
    n_slots = n_blocks * bm
    tok = jnp.repeat(jnp.arange(t, dtype=jnp.int32), TOP_K)
    slot_tok = (jnp.arange(n_slots, dtype=jnp.int32) % t).at[dest].set(tok)
    block_expert = jnp.minimum(
        jnp.searchsorted(pad_end, jnp.arange(n_blocks, dtype=jnp.int32) * bm, side="right"),
        N_EXPERTS - 1).astype(jnp.int32)
    n_used = (pad_end[-1] // bm).astype(jnp.int32).reshape(1)
    return slot_tok, gate, block_expert, n_used, dest.reshape(t, TOP_K)


def _rope_tables(pos, half):
    inv_freq = ROPE_BASE ** (-jnp.arange(half, dtype=F32) / half)
    ang = pos.astype(F32)[:, None] * inv_freq[None, :]
    return jnp.cos(ang), jnp.sin(ang)


def _z_layout(d):
    segs = [("ga", d), ("gb", d), ("gc", d),
            ("rv", RET_HEADS * RET_DV), ("rg", RET_HEADS * RET_DV),
            ("cb", CONV_D), ("cc", CONV_D), ("cu", CONV_D),
            ("cq", Q_LORA), ("rq", RET_HEADS * RET_DK), ("rk", RET_HEADS * RET_DK),
            ("ckv", KV_LORA), ("kpe", LANES)]
    offs, o = {}, 0
    for name, w in segs:
        assert o % w == 0
        offs[name] = o
        o += w
    return segs, offs, o


IN_PROJ_TN = 256


def _in_proj_tiles(d):
    src_names = ["cq", "ckv", "kpe", "rq", "rk", "rv", "rg", "cb", "cc", "cu", "ga", "gb", "gc"]
    src_w = [Q_LORA, KV_LORA, MLA_ROPE, RET_HEADS * RET_DK, RET_HEADS * RET_DK, RET_HEADS * RET_DV,
             RET_HEADS * RET_DV, CONV_D, CONV_D, CONV_D, d, d, d]
    src_off = dict(zip(src_names, np.cumsum([0] + src_w[:-1]).tolist()))
    segs, offs, total = _z_layout(d)
    tn = IN_PROJ_TN
    tiles = []
    for name, w in segs[:-2]:
        assert w % tn == 0
        tiles += [src_off[name] + i * tn for i in range(w // tn)]
    assert [n for n, _ in segs[-2:]] == ["ckv", "kpe"] and src_off["kpe"] == src_off["ckv"] + KV_LORA
    assert KV_LORA + LANES == tn and src_off["ckv"] + tn <= sum(src_w)
    tiles.append(src_off["ckv"])
    assert len(tiles) * tn == total
    return tiles, offs


def kernel(x_prompt, x_sample, cache_ckv, cache_kpe, state_ret, state_conv, page_table, norm_mix_g, w_in, mla_g_cq, mla_g_ckv, mla_w_uq, mla_w_uk, mla_w_uv, mla_g_qn, mla_g_kn, ret_g, conv_w, w_branch_a, w_branch_b, w_branch_c, w_out, norm_ffn_g, router_group_w, router_group_b, router_expert_w, router_expert_b, expert_w_gate, expert_w_up, expert_w_down):
    nb, s, d = x_prompt.shape
    ns, dq, _ = x_sample.shape
    depth = w_in.shape[0]
    tp, ts = nb * s, ns * dq
    past_len = page_table.shape[1] * PAGE_SIZE

    x = jnp.concatenate([x_prompt.reshape(tp, d), x_sample.reshape(ts, d)], axis=0)
    pos = jnp.concatenate([jnp.tile(jnp.arange(s, dtype=jnp.int32), nb),
                           jnp.tile(past_len + jnp.arange(dq, dtype=jnp.int32), ns)])
    cm, sm = _rope_tables(pos, MLA_ROPE // 2)
    zm = jnp.zeros_like(sm)
    mla_tabs = (jnp.concatenate([cm, cm, cm, cm], axis=1),
                jnp.concatenate([-sm, zm, -sm, zm], axis=1),
                jnp.concatenate([zm, sm, zm, sm], axis=1))
    cr, sr = _rope_tables(pos, RET_DK // 2)
    ret_tabs = (jnp.concatenate([cr, cr], axis=1), jnp.concatenate([-sr, sr], axis=1))

    uq_perm = np.concatenate(
        [h * MLA_QK + np.arange(MLA_NOPE) for h in range(MLA_HEADS)]
        + [h * MLA_QK + MLA_NOPE + np.arange(MLA_ROPE) for h in range(MLA_HEADS)])

    outs = {k: [] for k in ("ckv_p", "kpe_p", "ret_p", "conv_p", "ckv_s", "kpe_s", "ret_s", "conv_s")}
    w_in_t = jnp.swapaxes(w_in, 1, 2)
    cache_kpe_t = jnp.swapaxes(cache_kpe, 2, 3)
    in_tiles, offs = _in_proj_tiles(d)
    for l in range(depth):
        wuq_p = mla_w_uq[l][:, uq_perm].astype(BF)
        wukt = mla_w_uk[l].T.astype(BF)
        wuv = mla_w_uv[l].astype(BF)
        gqn = mla_g_qn[l][:MLA_NOPE].reshape(1, -1)
        gqr = jnp.tile(mla_g_qn[l][MLA_NOPE:], 2).reshape(1, -1)
        gkn = mla_g_kn[l][:MLA_NOPE].reshape(1, -1)
        gkr = jnp.tile(mla_g_kn[l][MLA_NOPE:], 2).reshape(1, -1)

        xn = _norm_cast(x, norm_mix_g[l])
        z = _in_proj(xn, w_in_t, l, in_tiles, IN_PROJ_TN)

        c_lat, k_rot, q, kcat, rs = _mla_prep(
            z, offs, mla_tabs, mla_g_cq[l].reshape(1, -1), mla_g_ckv[l].reshape(1, -1),
            wuq_p, wukt, gqn, gqr, gkn, gkr)
        oa_p = _attn_prompt(q, kcat, rs, wuv, nb, s)
        oa_s = _attn_paged(page_table, q, c_lat[tp:].reshape(ns, dq, KV_LORA),
                           jnp.swapaxes(k_rot[tp:].reshape(ns, dq, MLA_ROPE), 1, 2), wukt, wuv,
                           cache_ckv, cache_kpe_t, l, tp // dq)
        oa = jnp.concatenate([oa_p, oa_s.reshape(ts, -1).astype(BF)], axis=0)

        ob_p, ret_p = _retention(z, offs, ret_tabs, ret_g[l], nb, s, 0, None)
        ob_s, ret_s = _retention(z, offs, ret_tabs, ret_g[l], ns, dq, tp, state_ret, l)
        ob = jnp.concatenate([ob_p, ob_s], axis=0)

        oc_p, conv_p = _short_conv(z, offs, conv_w[l], nb, s, 0, None)
        oc_s, conv_s = _short_conv(z, offs, conv_w[l], ns, dq, tp, state_conv[l])
        oc = jnp.concatenate([oc_p, oc_s], axis=0)

        m = _merge(oa, ob, oc, w_branch_a, w_branch_b, w_branch_c, l, z, offs, d)
        x = _matmul(m, w_out, l, residual=x, tm_target=1024, tn_target=512, name="out_proj")

        wr = jnp.concatenate([router_group_w[l], router_expert_w[l]], axis=1)
        wr = jnp.pad(wr, ((0, 0), (0, LANES - wr.shape[1])))
        wr_hi = wr.astype(BF)
        wr_lo = (wr - wr_hi.astype(F32)).astype(BF)
        hn, logits = _norm_router(x, norm_ffn_g[l], wr_hi, wr_lo)
        slot_tok, gate, block_expert, n_used, dest = _route(logits, router_group_b[l], router_expert_b[l])
        y = _experts(hn, slot_tok, block_expert, n_used,
                     expert_w_gate, expert_w_up, expert_w_down, l)
        x = _combine(x, y, dest, gate)

        outs["ckv_p"].append(c_lat[:tp].reshape(nb, s, KV_LORA))
        outs["kpe_p"].append(k_rot[:tp].reshape(nb, s, MLA_ROPE))
        outs["ret_p"].append(ret_p)
        outs["conv_p"].append(conv_p)
        outs["ckv_s"].append(c_lat[tp:].reshape(ns, dq, KV_LORA))
        outs["kpe_s"].append(k_rot[tp:].reshape(ns, dq, MLA_ROPE))
        outs["ret_s"].append(ret_s)
        outs["conv_s"].append(conv_s)

    st = lambda k: jnp.stack(outs[k])
    return (x[:tp].reshape(nb, s, d), x[tp:].reshape(ns, dq, d),
            st("ckv_p"), st("kpe_p"), st("ret_p"), st("conv_p"),
            st("ckv_s"), st("kpe_s"), st("ret_s"), st("conv_s"))
```

```python
import functools
import math

import numpy as np
import jax
import jax.numpy as jnp
from jax import lax
from jax.experimental import pallas as pl
from jax.experimental.pallas import tpu as pltpu

BF = jnp.bfloat16
F32 = jnp.float32

MLA_HEADS = 8
MLA_NOPE = 128
MLA_ROPE = 64
MLA_QK = MLA_NOPE + MLA_ROPE
MLA_V = 128
Q_LORA = 512
KV_LORA = 128
RET_HEADS = 4
RET_DK = 128
RET_DV = 256
RET_CHUNK = 128
CONV_D = 1024
CONV_W = 3
N_GROUPS = 8
EXPERTS_PER_GROUP = 8
N_EXPERTS = N_GROUPS * EXPERTS_PER_GROUP
TOP_K = 2
EXPERT_FF = 512
PAGE_SIZE = 128
ROPE_BASE = 10000.0
NORM_EPS = 1e-6
ATTN_SCALE = MLA_QK ** -0.5
LOG_G = tuple(math.log1p(-(2.0 ** (-5.0 - h))) for h in range(RET_HEADS))

LANES = 128
QPAD = 256
VMEM_LIMIT = 56 * 1024 * 1024
MOE_BM = 256
NEG_INF = float("-inf")

_NT = (((1,), (1,)), ((), ()))


def _cparams(sem, vmem=VMEM_LIMIT):
    return pltpu.CompilerParams(dimension_semantics=sem, vmem_limit_bytes=vmem)


def _pick(n, target, mult=8):
    if n <= target:
        return n
    for d in range(target, 0, -1):
        if n % d == 0 and d % mult == 0:
            return d
    return n


def _rms(x, g):
    return x * lax.rsqrt(jnp.mean(x * x, axis=-1, keepdims=True) + NORM_EPS) * g


def _lane_sumsq_t(x):
    sq = x * x
    hi = sq.astype(BF)
    r1 = sq - hi.astype(F32)
    mid = r1.astype(BF)
    lo = (r1 - mid.astype(F32)).astype(BF)
    ones = jnp.ones((8, x.shape[1]), BF)
    nt = lambda b: lax.dot_general(ones, b, _NT, preferred_element_type=F32)
    return nt(hi) + (nt(mid) + nt(lo))


def _norm_kernel(x_ref, g_ref, o_ref):
    o_ref[...] = _rms(x_ref[...], g_ref[...]).astype(o_ref.dtype)


def _norm_cast(x, g):
    t, d = x.shape
    tm = _pick(t, 512)
    return pl.pallas_call(
        _norm_kernel,
        out_shape=jax.ShapeDtypeStruct((t, d), BF),
        grid=(t // tm,),
        in_specs=[pl.BlockSpec((tm, d), lambda i: (i, 0)), pl.BlockSpec((1, d), lambda i: (0, 0))],
        out_specs=pl.BlockSpec((tm, d), lambda i: (i, 0)),
        compiler_params=_cparams(("parallel",)),
        name="norm_cast",
    )(x, g.reshape(1, d))


def _mm_kernel(a_ref, w_ref, o_ref):
    o_ref[...] = jnp.dot(a_ref[...].astype(BF), w_ref[...].astype(BF),
                         preferred_element_type=F32).astype(o_ref.dtype)


def _mm_res_kernel(a_ref, w_ref, r_ref, o_ref):
    o_ref[...] = r_ref[...] + jnp.dot(a_ref[...].astype(BF), w_ref[...].astype(BF),
                                      preferred_element_type=F32)


def _in_proj_kernel(offs_ref, a_ref, w_ref, o_ref):
    del offs_ref
    o_ref[...] = lax.dot_general(a_ref[...], w_ref[0].astype(BF), _NT, preferred_element_type=F32)


def _in_proj(xn, w_in_t, layer, tile_rows, tn):
    m, k = xn.shape
    tm = _pick(m, 2304, 16)
    nt = len(tile_rows)
    unit = math.gcd(*tile_rows, tn)
    assert unit % 8 == 0
    offs = jnp.asarray([r // unit for r in tile_rows], jnp.int32)
    return pl.pallas_call(
        _in_proj_kernel,
        out_shape=jax.ShapeDtypeStruct((m, nt * tn), F32),
        grid_spec=pltpu.PrefetchScalarGridSpec(
            num_scalar_prefetch=1,
            grid=(m // tm, nt),
            in_specs=[pl.BlockSpec((tm, k), lambda i, j, offs: (i, 0)),
                      pl.BlockSpec((pl.Element(1), pl.Element(tn), pl.Element(k)),
                                   lambda i, j, offs: (layer, offs[j] * unit, 0))],
            out_specs=pl.BlockSpec((tm, tn), lambda i, j, offs: (i, j))),
        compiler_params=_cparams(("parallel", "arbitrary")),
        name="in_proj",
    )(offs, xn, w_in_t)


def _matmul(a, w3, layer, residual=None, tm_target=2304, tn_target=256, name="matmul"):
    m, k = a.shape
    n = w3.shape[2]
    tm = _pick(m, tm_target, 16)
    tn = _pick(n, tn_target, LANES)
    in_specs = [pl.BlockSpec((tm, k), lambda i, j: (i, 0)),
                pl.BlockSpec((None, k, tn), lambda i, j: (layer, 0, j))]
    args = [a, w3]
    body = _mm_kernel
    if residual is not None:
        in_specs.append(pl.BlockSpec((tm, tn), lambda i, j: (i, j)))
        args.append(residual)
        body = _mm_res_kernel
    return pl.pallas_call(
        body,
        out_shape=jax.ShapeDtypeStruct((m, n), F32),
        grid=(m // tm, n // tn),
        in_specs=in_specs,
        out_specs=pl.BlockSpec((tm, tn), lambda i, j: (i, j)),
        compiler_params=_cparams(("parallel", "arbitrary")),
        name=name,
    )(*args)


def _prep_kernel(cq_ref, ckv_ref, kpe_ref, c_ref, s1_ref, s2_ref, gcq_ref, gckv_ref, wuq_ref, wukt_ref,
                 gqn_ref, gqr_ref, gkn_ref, gkr_ref,
                 clat_ref, krot_ref, q_ref, kcat_ref, rs_ref):
    tm = cq_ref.shape[0]
    cs, s1, s2 = c_ref[...], s1_ref[...], s2_ref[...]

    def rope64(x):
        return x * cs + pltpu.roll(x, 96, 1) * s1 + pltpu.roll(x, 32, 1) * s2

    low = lax.broadcasted_iota(jnp.int32, (tm, LANES), 1) < MLA_ROPE

    c_lat = _rms(ckv_ref[...], gckv_ref[...])
    clat_ref[...] = c_lat
    k_rot = jnp.where(low, rope64(kpe_ref[...]), 0.0)
    krot_ref[...] = k_rot[:, :MLA_ROPE]
    clb = c_lat.astype(BF)
    kcat_ref[:, 0:KV_LORA] = clb
    kcat_ref[:, KV_LORA:QPAD] = k_rot.astype(BF)

    knt = lax.dot_general(wukt_ref[...], clb, _NT, preferred_element_type=F32)
    ssq = jnp.sum((knt * knt).reshape(MLA_HEADS, MLA_NOPE, tm), axis=1)
    rs_ref[...] = lax.rsqrt((ssq + _lane_sumsq_t(k_rot)) / MLA_QK + NORM_EPS)

    cqn = _rms(cq_ref[...], gcq_ref[...]).astype(BF)
    q = jnp.dot(cqn, wuq_ref[...], preferred_element_type=F32)
    gqn, gqr, gkn, gkr = gqn_ref[...], gqr_ref[...], gkn_ref[...], gkr_ref[...]
    nope_w = MLA_HEADS * MLA_NOPE
    for j in range(MLA_HEADS // 2):
        xr = rope64(q[:, nope_w + LANES * j: nope_w + LANES * (j + 1)])
        sq = xr * xr
        ss = (jnp.sum(jnp.where(low, sq, 0.0), axis=-1, keepdims=True),
              jnp.sum(jnp.where(low, 0.0, sq), axis=-1, keepdims=True))
        for e in range(2):
            h = 2 * j + e
            nope = q[:, MLA_NOPE * h: MLA_NOPE * (h + 1)]
            r = lax.rsqrt((jnp.sum(nope * nope, axis=-1, keepdims=True) + ss[e]) / MLA_QK + NORM_EPS)
            qg = (nope * r * gqn * gkn).astype(BF)
            q_ref[h, :, 0:KV_LORA] = jnp.dot(qg, wukt_ref[MLA_NOPE * h: MLA_NOPE * (h + 1), :],
                                             preferred_element_type=F32)
            rp = xr * r * gqr * gkr
            if e == 1:
                rp = pltpu.roll(rp, MLA_ROPE, 1)
            q_ref[h, :, KV_LORA:QPAD] = jnp.where(low, rp, 0.0)


def _mla_prep(z, offs, tabs, gcq, gckv, wuq_p, wukt, gqn, gqr, gkn, gkr):
    t = z.shape[0]
    tm = _pick(t, 256, 16)
    cs, s1, s2 = tabs

    def zs(width, off):
        return pl.BlockSpec((tm, width), lambda i: (i, off // width))

    row = lambda w: pl.BlockSpec((tm, w), lambda i: (i, 0))
    full = lambda a: pl.BlockSpec(a.shape, lambda i: (0,) * a.ndim)
    small = [gcq, gckv, wuq_p, wukt, gqn, gqr, gkn, gkr]
    return pl.pallas_call(
        _prep_kernel,
        out_shape=(jax.ShapeDtypeStruct((t, KV_LORA), F32),
                   jax.ShapeDtypeStruct((t, MLA_ROPE), F32),
                   jax.ShapeDtypeStruct((MLA_HEADS, t, QPAD), F32),
                   jax.ShapeDtypeStruct((t, QPAD), BF),
                   jax.ShapeDtypeStruct((MLA_HEADS, t), F32)),
        grid=(t // tm,),
        in_specs=[zs(Q_LORA, offs["cq"]), zs(KV_LORA, offs["ckv"]), zs(LANES, offs["kpe"]),
                  row(LANES), row(LANES), row(LANES)] + [full(a) for a in small],
        out_specs=(row(KV_LORA), row(MLA_ROPE),
                   pl.BlockSpec((MLA_HEADS, tm, QPAD), lambda i: (0, i, 0)),
                   row(QPAD),
                   pl.BlockSpec((MLA_HEADS, tm), lambda i: (0, i))),
        compiler_params=_cparams(("parallel",)),
        name="mla_prep",
    )(z, z, z, cs, s1, s2, *small)


PROMPT_HEAD_GROUPS = 4


def _attn_prompt_kernel(q_ref, k_ref, rs_ref, wuv_ref, o_ref, qb_sc, m_sc, l_sc, acc_sc, *, bq, bk):
    qi = pl.program_id(1)
    ki = pl.program_id(2)
    rows = MLA_HEADS * bq

    @pl.when(ki == 0)
    def _():
        qb_sc[...] = q_ref[...].reshape(rows, QPAD).astype(BF)
        m_sc[...] = jnp.full(m_sc.shape, NEG_INF, F32)
        l_sc[...] = jnp.zeros(l_sc.shape, F32)
        acc_sc[...] = jnp.zeros(acc_sc.shape, F32)

    def update(masked):
        k = k_ref[...]
        rs = rs_ref[...] * ATTN_SCALE
        hg = MLA_HEADS // PROMPT_HEAD_GROUPS
        gr = hg * bq
        for g in range(PROMPT_HEAD_GROUPS):
            rsl = slice(g * gr, (g + 1) * gr)
            s = lax.dot_general(qb_sc[rsl, :], k, _NT, preferred_element_type=F32)
            s3 = s.reshape(hg, bq, bk) * rs[g * hg:(g + 1) * hg, None, :]
            if masked:
                qpos = qi * bq + lax.broadcasted_iota(jnp.int32, (1, bq, bk), 1)
                kpos = ki * bk + lax.broadcasted_iota(jnp.int32, (1, bq, bk), 2)
                s3 = jnp.where(kpos <= qpos, s3, NEG_INF)
            s2 = s3.reshape(gr, bk)
            m_prev = m_sc[rsl, :]
            m_new = jnp.maximum(m_prev, jnp.max(s2, axis=-1, keepdims=True))
            alpha = jnp.exp(m_prev - m_new)
            p = jnp.exp(s2 - m_new)
            l_sc[rsl, :] = alpha * l_sc[rsl, :] + jnp.sum(p, axis=-1, keepdims=True)
            acc_sc[rsl, :] = alpha * acc_sc[rsl, :] + jnp.dot(p.astype(BF), k[:, 0:KV_LORA],
                                                              preferred_element_type=F32)
            m_sc[rsl, :] = m_new

    first_q, last_q = qi * bq, qi * bq + (bq - 1)
    last_k = ki * bk + (bk - 1)

    @pl.when(last_k <= first_q)
    def _():
        update(False)

    @pl.when((last_k > first_q) & (ki * bk <= last_q))
    def _():
        update(True)

    @pl.when(ki == pl.num_programs(2) - 1)
    def _():
        ol = (acc_sc[...] / l_sc[...]).astype(BF)
        for h in range(MLA_HEADS):
            o_ref[:, MLA_V * h: MLA_V * (h + 1)] = jnp.dot(
                ol[h * bq:(h + 1) * bq], wuv_ref[:, MLA_V * h: MLA_V * (h + 1)],
                preferred_element_type=F32).astype(o_ref.dtype)


def _attn_prompt(q, kcat, rs, wuv, nb, s):
    bq = _pick(s, 128, 16)
    bk = _pick(s, 512, LANES)
    nq, nk = s // bq, s // bk
    rows = MLA_HEADS * bq

    def kblk(b, qi, ki):
        return b * nk + jnp.minimum(ki, (qi * bq + bq - 1) // bk)

    return pl.pallas_call(
        functools.partial(_attn_prompt_kernel, bq=bq, bk=bk),
        out_shape=jax.ShapeDtypeStruct((nb * s, MLA_HEADS * MLA_V), BF),
        grid=(nb, nq, nk),
        in_specs=[pl.BlockSpec((MLA_HEADS, bq, QPAD), lambda b, qi, ki: (0, b * nq + qi, 0)),
                  pl.BlockSpec((bk, QPAD), lambda b, qi, ki: (kblk(b, qi, ki), 0)),
                  pl.BlockSpec((MLA_HEADS, bk), lambda b, qi, ki: (0, kblk(b, qi, ki))),
                  pl.BlockSpec(wuv.shape, lambda b, qi, ki: (0, 0))],
        out_specs=pl.BlockSpec((bq, MLA_HEADS * MLA_V), lambda b, qi, ki: (b * nq + qi, 0)),
        scratch_shapes=[pltpu.VMEM((rows, QPAD), BF), pltpu.VMEM((rows, 1), F32),
                        pltpu.VMEM((rows, 1), F32), pltpu.VMEM((rows, KV_LORA), F32)],
        compiler_params=_cparams(("parallel", "parallel", "arbitrary")),
        name="attn_prompt",
    )(q, kcat, rs, wuv)


def _attn_paged_kernel(pt_ref, q_ref, cnew_ref, krnew_ref, wukt_ref, wuv_ref, ckv_hbm, kpe_hbm, o_ref,
                       wq_sc, qr_sc, m_sc, l_sc, acc_sc, cpad_sc, krpad_sc, sprev_sc, cbprev_sc,
                       cbuf, krbuf, sem, *, pp, dq, layer):
    j = pl.program_id(1)
    step = pl.program_id(0) * pl.num_programs(1) + j
    last_step = pl.num_programs(0) * pl.num_programs(1) - 1
    slot = step % 2
    rows = MLA_HEADS * dq
    nk = MLA_HEADS * MLA_NOPE

    def page_copies(src_step, dst_slot):
        cps = []
        for i in range(pp):
            page = pt_ref[src_step * pp + i]
            cps.append(pltpu.make_async_copy(ckv_hbm.at[layer, page], cbuf.at[dst_slot, i], sem.at[dst_slot]))
            cps.append(pltpu.make_async_copy(kpe_hbm.at[layer, page], krbuf.at[dst_slot, i], sem.at[dst_slot]))
        return cps

    def fetch_next_and_wait():
        for cp in page_copies(jnp.minimum(step + 1, last_step), 1 - slot):
            cp.start()
        for cp in page_copies(step, slot):
            cp.wait()

    @pl.when(step == 0)
    def _():
        for cp in page_copies(0, 0):
            cp.start()

    @pl.when(j == 0)
    def _():
        q = q_ref[...].reshape(rows, QPAD)
        wq_sc[0:nk, :] = wukt_ref[...]
        wq_sc[nk:nk + rows, :] = q[:, 0:KV_LORA].astype(BF)
        qr_sc[...] = q[:, KV_LORA:KV_LORA + MLA_ROPE].astype(BF)
        m_sc[...] = jnp.full(m_sc.shape, NEG_INF, F32)
        l_sc[...] = jnp.zeros(l_sc.shape, F32)
        acc_sc[...] = jnp.zeros(acc_sc.shape, F32)

    def scores(c, krt):
        n = c.shape[0]
        cb = c.astype(BF)
        big = lax.dot_general(wq_sc[...], cb, _NT, preferred_element_type=F32)
        knt = big[0:nk]
        ssq = jnp.sum((knt * knt).reshape(MLA_HEADS, MLA_NOPE, n), axis=1)
        krsq = jnp.sum(krt * krt, axis=0, keepdims=True)
        rs = lax.rsqrt((ssq + krsq) / MLA_QK + NORM_EPS) * ATTN_SCALE
        s = big[nk:nk + rows] + jnp.dot(qr_sc[...], krt.astype(BF), preferred_element_type=F32)
        return (s.reshape(MLA_HEADS, dq, n) * rs[:, None, :]).reshape(rows, n), cb

    def softmax_update(s, cb):
        m_prev = m_sc[...]
        m_new = jnp.maximum(m_prev, jnp.max(s, axis=-1, keepdims=True))
        alpha = jnp.exp(m_prev - m_new)
        p = jnp.exp(s - m_new)
        l_sc[...] = alpha * l_sc[...] + jnp.sum(p, axis=-1, keepdims=True)
        acc_sc[...] = alpha * acc_sc[...] + jnp.dot(p.astype(BF), cb, preferred_element_type=F32)
        m_sc[...] = m_new

    def page_chunk():
        return (jnp.concatenate([cbuf[slot, i] for i in range(pp)], axis=0),
                jnp.concatenate([krbuf[slot, i] for i in range(pp)], axis=1))

    @pl.when(j == 0)
    def _():
        fetch_next_and_wait()
        s, cb = scores(*page_chunk())
        sprev_sc[...] = s
        cbprev_sc[...] = cb
        cpad_sc[...] = jnp.zeros(cpad_sc.shape, F32)
        krpad_sc[...] = jnp.zeros(krpad_sc.shape, F32)
        cpad_sc[0:dq, :] = cnew_ref[...]
        krpad_sc[:, 0:dq] = krnew_ref[...]
        t = lax.broadcasted_iota(jnp.int32, (rows, PAGE_SIZE), 1)
        r = lax.broadcasted_iota(jnp.int32, (MLA_HEADS, dq, PAGE_SIZE), 1).reshape(rows, PAGE_SIZE)
        s_new, cb_new = scores(cpad_sc[...], krpad_sc[...])
        softmax_update(jnp.where(t <= r, s_new, NEG_INF), cb_new)

    @pl.when(j > 0)
    def _():
        fetch_next_and_wait()
        s_prev, cb_prev = sprev_sc[...], cbprev_sc[...]
        s, cb = scores(*page_chunk())
        softmax_update(s_prev, cb_prev)
        sprev_sc[...] = s
        cbprev_sc[...] = cb

    @pl.when(step == last_step)
    def _():
        for cp in page_copies(last_step, 1 - slot):
            cp.wait()

    @pl.when(j == pl.num_programs(1) - 1)
    def _():
        softmax_update(sprev_sc[...], cbprev_sc[...])
        ol = (acc_sc[...] / l_sc[...]).astype(BF)
        full = jnp.dot(ol, wuv_ref[...], preferred_element_type=F32)
        for h in range(MLA_HEADS):
            o_ref[:, MLA_V * h: MLA_V * (h + 1)] = full[h * dq:(h + 1) * dq, MLA_V * h: MLA_V * (h + 1)]


PAGES_PER_STEP = 16


def _attn_paged(page_table, q, c_new, kr_new_t, wukt, wuv, cache_ckv, cache_kpe_t, layer, row_blk0):
    nseq, n_pages = page_table.shape
    dq = c_new.shape[1]
    pp = math.gcd(n_pages, PAGES_PER_STEP)
    n_chunks = n_pages // pp
    pt = page_table.reshape(-1)

    in_specs = [pl.BlockSpec((MLA_HEADS, dq, QPAD), lambda b, j, pt_ref: (0, row_blk0 + b, 0)),
                pl.BlockSpec((None, dq, KV_LORA), lambda b, j, pt_ref: (b, 0, 0)),
                pl.BlockSpec((None, MLA_ROPE, dq), lambda b, j, pt_ref: (b, 0, 0)),
                pl.BlockSpec(wukt.shape, lambda b, j, pt_ref: (0, 0)),
                pl.BlockSpec(wuv.shape, lambda b, j, pt_ref: (0, 0)),
                pl.BlockSpec(memory_space=pl.ANY), pl.BlockSpec(memory_space=pl.ANY)]
    rows = MLA_HEADS * dq
    return pl.pallas_call(
        functools.partial(_attn_paged_kernel, pp=pp, dq=dq, layer=layer),
        out_shape=jax.ShapeDtypeStruct((nseq, dq, MLA_HEADS * MLA_V), F32),
        grid_spec=pltpu.PrefetchScalarGridSpec(
            num_scalar_prefetch=1,
            grid=(nseq, n_chunks),
            in_specs=in_specs,
            out_specs=pl.BlockSpec((None, dq, MLA_HEADS * MLA_V), lambda b, j, pt_ref: (b, 0, 0)),
            scratch_shapes=[pltpu.VMEM((MLA_HEADS * MLA_NOPE + rows, KV_LORA), BF),
                            pltpu.VMEM((rows, MLA_ROPE), BF),
                            pltpu.VMEM((rows, 1), F32), pltpu.VMEM((rows, 1), F32),
                            pltpu.VMEM((rows, KV_LORA), F32),
                            pltpu.VMEM((PAGE_SIZE, KV_LORA), F32), pltpu.VMEM((MLA_ROPE, PAGE_SIZE), F32),
                            pltpu.VMEM((rows, pp * PAGE_SIZE), F32), pltpu.VMEM((pp * PAGE_SIZE, KV_LORA), BF),
                            pltpu.VMEM((2, pp, PAGE_SIZE, KV_LORA), F32),
                            pltpu.VMEM((2, pp, MLA_ROPE, PAGE_SIZE), F32),
                            pltpu.SemaphoreType.DMA((2,))]),
        compiler_params=_cparams(("arbitrary", "arbitrary")),
        name="attn_paged",
    )(pt, q, c_new, kr_new_t, wukt, wuv, cache_ckv, cache_kpe_t)


def _ret_kernel(rq_ref, rk_ref, rv_ref, rg_ref, cf_ref, sf_ref, g_ref, *rest, chunk, sb, has_state):
    if has_state:
        st_ref, o_ref, so_ref, r_sc, kp_sc, kd_sc, vp_sc, o_sc = rest
    else:
        o_ref, so_ref, r_sc, kp_sc, kd_sc, vp_sc, o_sc = rest
    ci = pl.program_id(1)
    lp = kp_sc.shape[0]

    @pl.when(ci == 0)
    def _():
        if has_state:
            r_sc[...] = st_ref[...]
        else:
            r_sc[...] = jnp.zeros(r_sc.shape, F32)

    if chunk < lp:
        kp_sc[...] = jnp.zeros(kp_sc.shape, F32)
        kd_sc[...] = jnp.zeros(kd_sc.shape, F32)
        vp_sc[...] = jnp.zeros(vp_sc.shape, F32)

    def pad(x, sc):
        if chunk == lp:
            return x
        sc[0:chunk, :] = x
        return sc[...]

    t = lax.broadcasted_iota(jnp.int32, (chunk, 1), 0).astype(F32)
    rel = (lax.broadcasted_iota(jnp.int32, (chunk, lp), 0)
           - lax.broadcasted_iota(jnp.int32, (chunk, lp), 1)).astype(F32)
    for si in range(sb):
        rows = slice(si * chunk, (si + 1) * chunk)
        cf, sf = cf_ref[rows, :], sf_ref[rows, :]
        rope = lambda x: x * cf + pltpu.roll(x, RET_DK // 2, 1) * sf
        for h in range(RET_HEADS):
            lg = LOG_G[h]
            q = rope(rq_ref[rows, RET_DK * h: RET_DK * (h + 1)])
            k = rope(rk_ref[rows, RET_DK * h: RET_DK * (h + 1)]) * (RET_DK ** -0.5)
            v = rv_ref[rows, RET_DV * h: RET_DV * (h + 1)]
            qb = q.astype(BF)
            kp = pad(k, kp_sc).astype(BF)
            vp = pad(v, vp_sc).astype(BF)
            decay = jnp.where(rel >= 0.0, jnp.exp(lg * jnp.maximum(rel, 0.0)), 0.0)
            s = lax.dot_general(qb, kp, _NT, preferred_element_type=F32) * decay
            r_old = r_sc[si, h]
            o = jnp.dot(s.astype(BF), vp, preferred_element_type=F32)
            o = o + jnp.dot(qb, r_old.astype(BF), preferred_element_type=F32) * jnp.exp(lg * (t + 1.0))
            kd = pad(k * jnp.exp(lg * (chunk - 1.0 - t)), kd_sc)
            r_sc[si, h] = math.exp(lg * chunk) * r_old + jnp.dot(
                kd.T.astype(BF), vp, preferred_element_type=F32)
            on = _rms(o, g_ref[h:h + 1, :])
            rg = rg_ref[rows, RET_DV * h: RET_DV * (h + 1)]
            o_sc[rows, RET_DV * h: RET_DV * (h + 1)] = rg * jax.nn.sigmoid(rg) * on
    o_ref[...] = o_sc[...].astype(o_ref.dtype)

    @pl.when(ci == pl.num_programs(1) - 1)
    def _():
        so_ref[...] = r_sc[...]


def _retention(z, offs, tabs, ret_g, nseq, seq_len, row0, state, layer=0):
    chunk = RET_CHUNK if seq_len % RET_CHUNK == 0 else seq_len
    nc = seq_len // chunk
    sb = 1 if nc > 1 else _pick(nseq, max(1, 64 // chunk), 1)
    rows = sb * chunk
    lp = max(chunk, LANES)
    blk0 = row0 // rows
    cf, sf = tabs
    dk, dv = RET_HEADS * RET_DK, RET_HEADS * RET_DV

    def zs(width, off):
        return pl.BlockSpec((rows, width), lambda b, c: (blk0 + b * nc + c, off // width))

    tab = pl.BlockSpec((rows, RET_DK), lambda b, c: (blk0 + b * nc + c, 0))
    st_spec = pl.BlockSpec((sb, RET_HEADS, RET_DK, RET_DV), lambda b, c: (b, 0, 0, 0))
    in_specs = [zs(dk, offs["rq"]), zs(dk, offs["rk"]), zs(dv, offs["rv"]), zs(dv, offs["rg"]), tab, tab,
                pl.BlockSpec(ret_g.shape, lambda b, c: (0, 0))]
    args = [z, z, z, z, cf, sf, ret_g]
    if state is not None:
        in_specs.append(pl.BlockSpec((None, sb, RET_HEADS, RET_DK, RET_DV), lambda b, c: (layer, b, 0, 0, 0)))
        args.append(state)
    return pl.pallas_call(
        functools.partial(_ret_kernel, chunk=chunk, sb=sb, has_state=state is not None),
        out_shape=(jax.ShapeDtypeStruct((nseq * seq_len, dv), BF),
                   jax.ShapeDtypeStruct((nseq, RET_HEADS, RET_DK, RET_DV), F32)),
        grid=(nseq // sb, nc),
        in_specs=in_specs,
        out_specs=(pl.BlockSpec((rows, dv), lambda b, c: (b * nc + c, 0)), st_spec),
        scratch_shapes=[pltpu.VMEM((sb, RET_HEADS, RET_DK, RET_DV), F32),
                        pltpu.VMEM((lp, RET_DK), F32), pltpu.VMEM((lp, RET_DK), F32),
                        pltpu.VMEM((lp, RET_DV), F32), pltpu.VMEM((rows, dv), F32)],
        compiler_params=_cparams(("parallel", "arbitrary")),
        name="retention",
    )(*args)


def _conv_kernel(cb_ref, cc_ref, cu_ref, w_ref, *rest, sb, rows_per_seq, has_state):
    if has_state:
        buf_ref, o_ref, so_ref, carry_sc = rest
    else:
        o_ref, so_ref, carry_sc = rest
    ti = pl.program_id(1)
    ln = rows_per_seq
    cd = cb_ref.shape[1]

    @pl.when(ti == 0)
    def _():
        if has_state:
            carry_sc[:, 6:8, :] = buf_ref[...]
        else:
            carry_sc[...] = jnp.zeros(carry_sc.shape, F32)

    w2 = cc_ref[...] * cu_ref[...]
    w3 = w2.reshape(sb, ln, cd)
    r1 = pltpu.roll(w2, 1, 0).reshape(sb, ln, cd)
    r2 = pltpu.roll(w2, 2, 0).reshape(sb, ln, cd)
    prev = carry_sc[...]
    p1, p2 = prev[:, 7:8, :], prev[:, 6:7, :]
    tpos = lax.broadcasted_iota(jnp.int32, (sb, ln, cd), 1)
    x1 = jnp.where(tpos == 0, p1, r1)
    x2 = jnp.where(tpos == 0, p2, jnp.where(tpos == 1, p1, r2))
    cw = w_ref[...]
    y = cw[0:1, :] * x2 + cw[1:2, :] * x1 + cw[2:3, :] * w3
    o_ref[...] = (cb_ref[...] * y.reshape(sb * ln, cd)).astype(o_ref.dtype)
    tail = w3[:, ln - 2:ln, :]
    carry_sc[:, 6:8, :] = tail

    @pl.when(ti == pl.num_programs(1) - 1)
    def _():
        so_ref[...] = tail


def _short_conv(z, offs, conv_w, nseq, seq_len, row0, state):
    if seq_len > 512:
        sb, ln = 1, _pick(seq_len, 512)
    else:
        ln = seq_len
        sb = _pick(nseq, max(1, 256 // ln), 1)
    nt = seq_len // ln
    rows = sb * ln
    blk0 = row0 // rows

    def zs(off):
        return pl.BlockSpec((rows, CONV_D), lambda b, t: (blk0 + b * nt + t, off // CONV_D))

    st_spec = pl.BlockSpec((sb, CONV_W - 1, CONV_D), lambda b, t: (b, 0, 0))
    in_specs = [zs(offs["cb"]), zs(offs["cc"]), zs(offs["cu"]), pl.BlockSpec(conv_w.shape, lambda b, t: (0, 0))]
    args = [z, z, z, conv_w]
    if state is not None:
        in_specs.append(st_spec)
        args.append(state)
    return pl.pallas_call(
        functools.partial(_conv_kernel, sb=sb, rows_per_seq=ln, has_state=state is not None),
        out_shape=(jax.ShapeDtypeStruct((nseq * seq_len, CONV_D), BF),
                   jax.ShapeDtypeStruct((nseq, CONV_W - 1, CONV_D), F32)),
        grid=(nseq // sb, nt),
        in_specs=in_specs,
        out_specs=(pl.BlockSpec((rows, CONV_D), lambda b, t: (b * nt + t, 0)), st_spec),
        scratch_shapes=[pltpu.VMEM((sb, 8, CONV_D), F32)],
        compiler_params=_cparams(("parallel", "arbitrary")),
        name="short_conv",
    )(*args)


def _merge_kernel(oa_ref, ob_ref, oc_ref, wa_ref, wb_ref, wc_ref, ga_ref, gb_ref, gc_ref, o_ref):
    def br(o, w, g):
        return jax.nn.sigmoid(g[...]) * jnp.dot(o[...], w[...].astype(BF), preferred_element_type=F32)

    m = br(oa_ref, wa_ref, ga_ref) + br(ob_ref, wb_ref, gb_ref) + br(oc_ref, wc_ref, gc_ref)
    o_ref[...] = m.astype(o_ref.dtype)


def _merge(oa, ob, oc, wa, wb, wc, layer, z, offs, d):
    t = oa.shape[0]
    tm = _pick(t, 1024, 16)
    tn = _pick(d, 512, LANES)
    osp = lambda a: pl.BlockSpec((tm, a.shape[1]), lambda i, j: (i, 0))
    wsp = lambda w: pl.BlockSpec((None, w.shape[1], tn), lambda i, j: (layer, 0, j))
    gsp = lambda off: pl.BlockSpec((tm, tn), lambda i, j: (i, off // tn + j))
    return pl.pallas_call(
        _merge_kernel,
        out_shape=jax.ShapeDtypeStruct((t, d), BF),
        grid=(t // tm, d // tn),
        in_specs=[osp(oa), osp(ob), osp(oc), wsp(wa), wsp(wb), wsp(wc),
                  gsp(offs["ga"]), gsp(offs["gb"]), gsp(offs["gc"])],
        out_specs=pl.BlockSpec((tm, tn), lambda i, j: (i, j)),
        compiler_params=_cparams(("parallel", "arbitrary")),
        name="branch_merge",
    )(oa, ob, oc, wa, wb, wc, z, z, z)


def _norm_router_kernel(x_ref, g_ref, whi_ref, wlo_ref, h_ref, lg_ref):
    y = _rms(x_ref[...], g_ref[...])
    h_ref[...] = y
    yhi = y.astype(BF)
    ylo = (y - yhi.astype(F32)).astype(BF)
    whi = whi_ref[...]
    lg_ref[...] = (jnp.dot(yhi, whi, preferred_element_type=F32)
                   + (jnp.dot(ylo, whi, preferred_element_type=F32)
                      + jnp.dot(yhi, wlo_ref[...], preferred_element_type=F32)))


def _norm_router(x, g, whi, wlo):
    t, d = x.shape
    tm = _pick(t, 512)
    n = whi.shape[1]
    return pl.pallas_call(
        _norm_router_kernel,
        out_shape=(jax.ShapeDtypeStruct((t, d), F32), jax.ShapeDtypeStruct((t, n), F32)),
        grid=(t // tm,),
        in_specs=[pl.BlockSpec((tm, d), lambda i: (i, 0)), pl.BlockSpec((1, d), lambda i: (0, 0)),
                  pl.BlockSpec((d, n), lambda i: (0, 0)), pl.BlockSpec((d, n), lambda i: (0, 0))],
        out_specs=(pl.BlockSpec((tm, d), lambda i: (i, 0)), pl.BlockSpec((tm, n), lambda i: (i, 0))),
        compiler_params=_cparams(("parallel",)),
        name="norm_router",
    )(x, g.reshape(1, d), whi, wlo)


def _row_copy(src_hbm, dst, sem, src_row, dst_slot, dst_row):
    return pltpu.make_async_copy(src_hbm.at[pl.ds(src_row, 1), :],
                                 dst.at[dst_slot, pl.ds(dst_row, 1), :], sem.at[dst_slot])


def _gather_rows(idx_ref, src_hbm, dst, sem, slot, n):
    def body(r, carry):
        _row_copy(src_hbm, dst, sem, idx_ref[0, r], slot, r).start()
        return carry
    lax.fori_loop(0, n, body, 0, unroll=8)


def _wait_rows(src_hbm, dst, sem, slot, n):
    def body(r, carry):
        _row_copy(src_hbm, dst, sem, 0, slot, r).wait()
        return carry
    lax.fori_loop(0, n, body, 0, unroll=8)


def _expert_kernel(be_ref, nu_ref, chg_ref, ws_ref, nx_ref, tok_ref, tok_next_ref, h_hbm,
                   wg_hbm, wu_hbm, wd_hbm, y_ref,
                   xbuf, sem, wg_sc, wu_sc, wd_sc, wg_st, wu_st, wd_st, wsem, *, layer):
    b = pl.program_id(0)
    nu = nu_ref[0]
    slot = b % 2
    bm = xbuf.shape[1]

    def weight_copies(e, ws):
        return [pltpu.make_async_copy(wg_hbm.at[layer, e], wg_st.at[ws], wsem.at[ws]),
                pltpu.make_async_copy(wu_hbm.at[layer, e], wu_st.at[ws], wsem.at[ws]),
                pltpu.make_async_copy(wd_hbm.at[layer, e], wd_st.at[ws], wsem.at[ws])]

    @pl.when(b == 0)
    def _():
        _gather_rows(tok_ref, h_hbm, xbuf, sem, 0, bm)
        for cp in weight_copies(be_ref[0], 0):
            cp.start()

    @pl.when(b < nu)
    def _():
        _wait_rows(h_hbm, xbuf, sem, slot, bm)

        @pl.when(chg_ref[b] == 1)
        def _():
            ws = ws_ref[b]
            for cp in weight_copies(be_ref[b], ws):
                cp.wait()
            wg_sc[...] = wg_st[ws].astype(BF)
            wu_sc[...] = wu_st[ws].astype(BF)
            wd_sc[...] = wd_st[ws].astype(BF)

            @pl.when(nx_ref[b] >= 0)
            def _():
                for cp in weight_copies(nx_ref[b], 1 - ws):
                    cp.start()

        for r in range(bm):
            _row_copy(h_hbm, xbuf, sem, tok_next_ref[0, r], 1 - slot, r).start()

        x = xbuf[slot].astype(BF)
        a = jnp.dot(x, wg_sc[...], preferred_element_type=F32)
        u = jnp.dot(x, wu_sc[...], preferred_element_type=F32)
        hid = (a * jax.nn.sigmoid(a) * u).astype(BF)
        y_ref[...] = jnp.dot(hid, wd_sc[...], preferred_element_type=F32)

    @pl.when(b == nu - 1)
    def _():
        _wait_rows(h_hbm, xbuf, sem, 1 - slot, bm)

    @pl.when(b >= nu)
    def _():
        y_ref[...] = jnp.zeros(y_ref.shape, F32)


def _experts(hn, slot_tok, block_expert, n_used, wg, wu, wd, layer):
    n_blocks = block_expert.shape[0]
    bm = MOE_BM
    d = hn.shape[1]
    ff = wg.shape[3]
    tok3 = slot_tok.reshape(n_blocks, 1, bm)
    smem_blk = lambda f: pl.BlockSpec((None, 1, bm), f, memory_space=pltpu.SMEM)
    blk = jnp.arange(n_blocks, dtype=jnp.int32)
    prev = jnp.concatenate([jnp.full((1,), -1, jnp.int32), block_expert[:-1]])
    changed = (block_expert != prev) & (blk < n_used[0])
    wslot = ((jnp.cumsum(changed.astype(jnp.int32)) - 1) % 2).astype(jnp.int32)
    pos = jnp.where(changed, blk, n_blocks)
    nxt_pos = lax.cummin(jnp.concatenate([pos[1:], jnp.full((1,), n_blocks, jnp.int32)]), reverse=True)
    nxt = jnp.where(nxt_pos < n_blocks, block_expert[jnp.minimum(nxt_pos, n_blocks - 1)], -1).astype(jnp.int32)
    return pl.pallas_call(
        functools.partial(_expert_kernel, layer=layer),
        out_shape=jax.ShapeDtypeStruct((n_blocks * bm, d), F32),
        grid_spec=pltpu.PrefetchScalarGridSpec(
            num_scalar_prefetch=5,
            grid=(n_blocks,),
            in_specs=[smem_blk(lambda b, *_: (b, 0, 0)),
                      smem_blk(lambda b, *_: (jnp.minimum(b + 1, n_blocks - 1), 0, 0)),
                      pl.BlockSpec(memory_space=pl.ANY),
                      pl.BlockSpec(memory_space=pl.ANY), pl.BlockSpec(memory_space=pl.ANY),
                      pl.BlockSpec(memory_space=pl.ANY)],
            out_specs=pl.BlockSpec((bm, d), lambda b, *_: (b, 0)),
            scratch_shapes=[pltpu.VMEM((2, bm, d), F32), pltpu.SemaphoreType.DMA((2,)),
                            pltpu.VMEM((d, ff), BF), pltpu.VMEM((d, ff), BF), pltpu.VMEM((ff, d), BF),
                            pltpu.VMEM((2, d, ff), F32), pltpu.VMEM((2, d, ff), F32),
                            pltpu.VMEM((2, ff, d), F32), pltpu.SemaphoreType.DMA((2,))]),
        compiler_params=_cparams(("arbitrary",)),
        name="moe_experts",
    )(block_expert, n_used, changed.astype(jnp.int32), wslot, nxt, tok3, tok3, hn, wg, wu, wd)


def _combine_kernel(d_ref, d_next_ref, x_ref, g_ref, y_hbm, o_ref, ybuf, sem):
    i = pl.program_id(0)
    slot = i % 2
    n = ybuf.shape[1]
    tm = n // TOP_K

    @pl.when(i == 0)
    def _():
        _gather_rows(d_ref, y_hbm, ybuf, sem, 0, n)

    @pl.when(i + 1 < pl.num_programs(0))
    def _():
        _gather_rows(d_next_ref, y_hbm, ybuf, sem, 1 - slot, n)

    _wait_rows(y_hbm, ybuf, sem, slot, n)
    g = g_ref[...]
    o_ref[...] = x_ref[...] + (ybuf[slot, 0:tm, :] * g[:, 0:1] + ybuf[slot, tm:n, :] * g[:, 1:2])


def _combine(x, y, dest, gate):
    t, d = x.shape
    tm = _pick(t, 128)
    nt = t // tm
    d3 = dest.reshape(nt, tm, TOP_K).transpose(0, 2, 1).reshape(nt, 1, TOP_K * tm)
    smem_blk = lambda f: pl.BlockSpec((None, 1, TOP_K * tm), f, memory_space=pltpu.SMEM)
    return pl.pallas_call(
        _combine_kernel,
        out_shape=jax.ShapeDtypeStruct((t, d), F32),
        grid=(nt,),
        in_specs=[smem_blk(lambda i: (i, 0, 0)),
                  smem_blk(lambda i: (jnp.minimum(i + 1, nt - 1), 0, 0)),
                  pl.BlockSpec((tm, d), lambda i: (i, 0)),
                  pl.BlockSpec((tm, TOP_K), lambda i: (i, 0)),
                  pl.BlockSpec(memory_space=pl.ANY)],
        out_specs=pl.BlockSpec((tm, d), lambda i: (i, 0)),
        scratch_shapes=[pltpu.VMEM((2, TOP_K * tm, d), F32), pltpu.SemaphoreType.DMA((2,))],
        compiler_params=_cparams(("arbitrary",)),
        name="moe_combine",
    )(d3, d3, x, gate, y)


def _route(logits, bias_g, bias_e):
    t = logits.shape[0]
    g_logits = logits[:, :N_GROUPS] + bias_g
    e_logits = (logits[:, N_GROUPS:N_GROUPS + N_EXPERTS] + bias_e).reshape(t, N_GROUPS, EXPERTS_PER_GROUP)
    p_group = jax.nn.softmax(g_logits, axis=-1)
    grp = jnp.argmax(g_logits, axis=-1).astype(jnp.int32)
    gsel = grp[:, None] == jnp.arange(N_GROUPS, dtype=jnp.int32)[None, :]
    in_grp = jnp.sum(jnp.where(gsel[:, :, None], e_logits, 0.0), axis=1)
    eidx = jnp.arange(EXPERTS_PER_GROUP, dtype=jnp.int32)[None, :]
    i1 = jnp.argmax(in_grp, axis=-1).astype(jnp.int32)
    rest = jnp.where(eidx == i1[:, None], -jnp.inf, in_grp)
    i2 = jnp.argmax(rest, axis=-1).astype(jnp.int32)
    top_v = jnp.stack([jnp.max(in_grp, axis=-1), jnp.max(rest, axis=-1)], axis=-1)
    top_i = jnp.stack([i1, i2], axis=-1)
    gate = jax.nn.softmax(top_v, axis=-1) * jnp.max(p_group, axis=-1, keepdims=True)
    expert = grp[:, None] * EXPERTS_PER_GROUP + top_i

    bm = MOE_BM
    n_assign = t * TOP_K
    e_flat = expert.reshape(-1)
    onehot_b = e_flat[:, None] == jnp.arange(N_EXPERTS, dtype=jnp.int32)[None, :]
    onehot = onehot_b.astype(jnp.int32)
    rank_all = jnp.cumsum(onehot, axis=0)
    counts = rank_all[-1]
    padded = (counts + bm - 1) // bm * bm
    pad_end = jnp.cumsum(padded)
    pad_start = pad_end - padded
    dest = jnp.sum(jnp.where(onehot_b, rank_all - 1 + pad_start[None, :], 0), axis=1).astype(jnp.int32)
    n_blocks = (n_assign + N_EXPERTS * (bm - 1) + bm - 1) // bm
    n_slots = n_blocks * bm
    tok = jnp.repeat(jnp.arange(t, dtype=jnp.int32), TOP_K)
    slot_tok = (jnp.arange(n_slots, dtype=jnp.int32) % t).at[dest].set(tok)
    blk_start = jnp.arange(n_blocks, dtype=jnp.int32) * bm
    block_expert = jnp.minimum(
        jnp.sum((pad_end[None, :] <= blk_start[:, None]).astype(jnp.int32), axis=1),
        N_EXPERTS - 1).astype(jnp.int32)
    n_used = (pad_end[-1] // bm).astype(jnp.int32).reshape(1)
    return slot_tok, gate, block_expert, n_used, dest.reshape(t, TOP_K)


def _rope_tables(pos, half):
    inv_freq = ROPE_BASE ** (-jnp.arange(half, dtype=F32) / half)
    ang = pos.astype(F32)[:, None] * inv_freq[None, :]
    return jnp.cos(ang), jnp.sin(ang)


def _z_layout(d):
    segs = [("ga", d), ("gb", d), ("gc", d),
            ("rv", RET_HEADS * RET_DV), ("rg", RET_HEADS * RET_DV),
            ("cb", CONV_D), ("cc", CONV_D), ("cu", CONV_D),
            ("cq", Q_LORA), ("rq", RET_HEADS * RET_DK), ("rk", RET_HEADS * RET_DK),
            ("ckv", KV_LORA), ("kpe", LANES)]
    offs, o = {}, 0
    for name, w in segs:
        assert o % w == 0
        offs[name] = o
        o += w
    return segs, offs, o


IN_PROJ_TN = 256


def _in_proj_tiles(d):
    src_names = ["cq", "ckv", "kpe", "rq", "rk", "rv", "rg", "cb", "cc", "cu", "ga", "gb", "gc"]
    src_w = [Q_LORA, KV_LORA, MLA_ROPE, RET_HEADS * RET_DK, RET_HEADS * RET_DK, RET_HEADS * RET_DV,
             RET_HEADS * RET_DV, CONV_D, CONV_D, CONV_D, d, d, d]
    src_off = dict(zip(src_names, np.cumsum([0] + src_w[:-1]).tolist()))
    segs, offs, total = _z_layout(d)
    tn = IN_PROJ_TN
    tiles = []
    for name, w in segs[:-2]:
        assert w % tn == 0
        tiles += [src_off[name] + i * tn for i in range(w // tn)]
    assert [n for n, _ in segs[-2:]] == ["ckv", "kpe"] and src_off["kpe"] == src_off["ckv"] + KV_LORA
    assert KV_LORA + LANES == tn and src_off["ckv"] + tn <= sum(src_w)
    tiles.append(src_off["ckv"])
    assert len(tiles) * tn == total
    return tiles, offs


def kernel(x_prompt, x_sample, cache_ckv, cache_kpe, state_ret, state_conv, page_table, norm_mix_g, w_in, mla_g_cq, mla_g_ckv, mla_w_uq, mla_w_uk, mla_w_uv, mla_g_qn, mla_g_kn, ret_g, conv_w, w_branch_a, w_branch_b, w_branch_c, w_out, norm_ffn_g, router_group_w, router_group_b, router_expert_w, router_expert_b, expert_w_gate, expert_w_up, expert_w_down):
    nb, s, d = x_prompt.shape
    ns, dq, _ = x_sample.shape
    depth = w_in.shape[0]
    tp, ts = nb * s, ns * dq
    past_len = page_table.shape[1] * PAGE_SIZE

    x = jnp.concatenate([x_prompt.reshape(tp, d), x_sample.reshape(ts, d)], axis=0)
    pos = jnp.concatenate([jnp.tile(jnp.arange(s, dtype=jnp.int32), nb),
                           jnp.tile(past_len + jnp.arange(dq, dtype=jnp.int32), ns)])
    cm, sm = _rope_tables(pos, MLA_ROPE // 2)
    zm = jnp.zeros_like(sm)
    mla_tabs = (jnp.concatenate([cm, cm, cm, cm], axis=1),
                jnp.concatenate([-sm, zm, -sm, zm], axis=1),
                jnp.concatenate([zm, sm, zm, sm], axis=1))
    cr, sr = _rope_tables(pos, RET_DK // 2)
    ret_tabs = (jnp.concatenate([cr, cr], axis=1), jnp.concatenate([-sr, sr], axis=1))

    uq_perm = np.concatenate(
        [h * MLA_QK + np.arange(MLA_NOPE) for h in range(MLA_HEADS)]
        + [h * MLA_QK + MLA_NOPE + np.arange(MLA_ROPE) for h in range(MLA_HEADS)])

    outs = {k: [] for k in ("ckv_p", "kpe_p", "ret_p", "conv_p", "ckv_s", "kpe_s", "ret_s", "conv_s")}
    w_in_t = jnp.swapaxes(w_in, 1, 2)
    cache_kpe_t = jnp.swapaxes(cache_kpe, 2, 3)
    in_tiles, offs = _in_proj_tiles(d)
    for l in range(depth):
        wuq_p = mla_w_uq[l][:, uq_perm].astype(BF)
        wukt = mla_w_uk[l].T.astype(BF)
        wuv = mla_w_uv[l].astype(BF)
        gqn = mla_g_qn[l][:MLA_NOPE].reshape(1, -1)
        gqr = jnp.tile(mla_g_qn[l][MLA_NOPE:], 2).reshape(1, -1)
        gkn = mla_g_kn[l][:MLA_NOPE].reshape(1, -1)
        gkr = jnp.tile(mla_g_kn[l][MLA_NOPE:], 2).reshape(1, -1)

        xn = _norm_cast(x, norm_mix_g[l])
        z = _in_proj(xn, w_in_t, l, in_tiles, IN_PROJ_TN)

        c_lat, k_rot, q, kcat, rs = _mla_prep(
            z, offs, mla_tabs, mla_g_cq[l].reshape(1, -1), mla_g_ckv[l].reshape(1, -1),
            wuq_p, wukt, gqn, gqr, gkn, gkr)
        oa_p = _attn_prompt(q, kcat, rs, wuv, nb, s)
        oa_s = _attn_paged(page_table, q, c_lat[tp:].reshape(ns, dq, KV_LORA),
                           jnp.swapaxes(k_rot[tp:].reshape(ns, dq, MLA_ROPE), 1, 2), wukt, wuv,
                           cache_ckv, cache_kpe_t, l, tp // dq)
        oa = jnp.concatenate([oa_p, oa_s.reshape(ts, -1).astype(BF)], axis=0)

        ob_p, ret_p = _retention(z, offs, ret_tabs, ret_g[l], nb, s, 0, None)
        ob_s, ret_s = _retention(z, offs, ret_tabs, ret_g[l], ns, dq, tp, state_ret, l)
        ob = jnp.concatenate([ob_p, ob_s], axis=0)

        oc_p, conv_p = _short_conv(z, offs, conv_w[l], nb, s, 0, None)
        oc_s, conv_s = _short_conv(z, offs, conv_w[l], ns, dq, tp, state_conv[l])
        oc = jnp.concatenate([oc_p, oc_s], axis=0)

        m = _merge(oa, ob, oc, w_branch_a, w_branch_b, w_branch_c, l, z, offs, d)
        x = _matmul(m, w_out, l, residual=x, tm_target=1024, tn_target=512, name="out_proj")

        wr = jnp.concatenate([router_group_w[l], router_expert_w[l]], axis=1)
        wr = jnp.pad(wr, ((0, 0), (0, LANES - wr.shape[1])))
        wr_hi = wr.astype(BF)
        wr_lo = (wr - wr_hi.astype(F32)).astype(BF)
        hn, logits = _norm_router(x, norm_ffn_g[l], wr_hi, wr_lo)
        slot_tok, gate, block_expert, n_used, dest = _route(logits, router_group_b[l], router_expert_b[l])
        y = _experts(hn, slot_tok, block_expert, n_used,
                     expert_w_gate, expert_w_up, expert_w_down, l)
        x = _combine(x, y, dest, gate)

        outs["ckv_p"].append(c_lat[:tp].reshape(nb, s, KV_LORA))
        outs["kpe_p"].append(k_rot[:tp].reshape(nb, s, MLA_ROPE))
        outs["ret_p"].append(ret_p)
        outs["conv_p"].append(conv_p)
        outs["ckv_s"].append(c_lat[tp:].reshape(ns, dq, KV_LORA))
        outs["kpe_s"].append(k_rot[tp:].reshape(ns, dq, MLA_ROPE))
        outs["ret_s"].append(ret_s)
        outs["conv_s"].append(conv_s)

    st = lambda k: jnp.stack(outs[k])
    return (x[:tp].reshape(nb, s, d), x[tp:].reshape(ns, dq, d),
            st("ckv_p"), st("kpe_p"), st("ret_p"), st("conv_p"),
            st("ckv_s"), st("kpe_s"), st("ret_s"), st("conv_s"))
```

```python
import functools
import math

import numpy as np
import jax
import jax.numpy as jnp
from jax import lax
from jax.experimental import pallas as pl
from jax.experimental.pallas import tpu as pltpu

BF = jnp.bfloat16
F32 = jnp.float32

MLA_HEADS = 8
MLA_NOPE = 128
MLA_ROPE = 64
MLA_QK = MLA_NOPE + MLA_ROPE
MLA_V = 128
Q_LORA = 512
KV_LORA = 128
RET_HEADS = 4
RET_DK = 128
RET_DV = 256
RET_CHUNK = 128
CONV_D = 1024
CONV_W = 3
N_GROUPS = 8
EXPERTS_PER_GROUP = 8
N_EXPERTS = N_GROUPS * EXPERTS_PER_GROUP
TOP_K = 2
EXPERT_FF = 512
PAGE_SIZE = 128
ROPE_BASE = 10000.0
NORM_EPS = 1e-6
ATTN_SCALE = MLA_QK ** -0.5
LOG_G = tuple(math.log1p(-(2.0 ** (-5.0 - h))) for h in range(RET_HEADS))

LANES = 128
QPAD = 256
VMEM_LIMIT = 56 * 1024 * 1024
MOE_BM = 256
NEG_INF = float("-inf")

_NT = (((1,), (1,)), ((), ()))


def _cparams(sem, vmem=VMEM_LIMIT):
    return pltpu.CompilerParams(dimension_semantics=sem, vmem_limit_bytes=vmem)


def _pick(n, target, mult=8):
    if n <= target:
        return n
    for d in range(target, 0, -1):
        if n % d == 0 and d % mult == 0:
            return d
    return n


def _rms(x, g):
    return x * lax.rsqrt(jnp.mean(x * x, axis=-1, keepdims=True) + NORM_EPS) * g


def _lane_sumsq_t(x):
    sq = x * x
    hi = sq.astype(BF)
    r1 = sq - hi.astype(F32)
    mid = r1.astype(BF)
    lo = (r1 - mid.astype(F32)).astype(BF)
    ones = jnp.ones((8, x.shape[1]), BF)
    nt = lambda b: lax.dot_general(ones, b, _NT, preferred_element_type=F32)
    return nt(hi) + (nt(mid) + nt(lo))


def _norm_kernel(x_ref, g_ref, o_ref):
    o_ref[...] = _rms(x_ref[...], g_ref[...]).astype(o_ref.dtype)


def _norm_cast(x, g):
    t, d = x.shape
    tm = _pick(t, 512)
    return pl.pallas_call(
        _norm_kernel,
        out_shape=jax.ShapeDtypeStruct((t, d), BF),
        grid=(t // tm,),
        in_specs=[pl.BlockSpec((tm, d), lambda i: (i, 0)), pl.BlockSpec((1, d), lambda i: (0, 0))],
        out_specs=pl.BlockSpec((tm, d), lambda i: (i, 0)),
        compiler_params=_cparams(("parallel",)),
        name="norm_cast",
    )(x, g.reshape(1, d))


def _mm_kernel(a_ref, w_ref, o_ref):
    o_ref[...] = jnp.dot(a_ref[...].astype(BF), w_ref[...].astype(BF),
                         preferred_element_type=F32).astype(o_ref.dtype)


def _mm_res_kernel(a_ref, w_ref, r_ref, o_ref):
    o_ref[...] = r_ref[...] + jnp.dot(a_ref[...].astype(BF), w_ref[...].astype(BF),
                                      preferred_element_type=F32)


def _in_proj_kernel(offs_ref, a_ref, w_ref, o_ref):
    del offs_ref
    o_ref[...] = lax.dot_general(a_ref[...], w_ref[0].astype(BF), _NT, preferred_element_type=F32)


def _in_proj(xn, w_in_t, layer, tile_rows, tn):
    m, k = xn.shape
    tm = _pick(m, 2304, 16)
    nt = len(tile_rows)
    unit = math.gcd(*tile_rows, tn)
    assert unit % 8 == 0
    offs = jnp.asarray([r // unit for r in tile_rows], jnp.int32)
    return pl.pallas_call(
        _in_proj_kernel,
        out_shape=jax.ShapeDtypeStruct((m, nt * tn), F32),
        grid_spec=pltpu.PrefetchScalarGridSpec(
            num_scalar_prefetch=1,
            grid=(m // tm, nt),
            in_specs=[pl.BlockSpec((tm, k), lambda i, j, offs: (i, 0)),
                      pl.BlockSpec((pl.Element(1), pl.Element(tn), pl.Element(k)),
                                   lambda i, j, offs: (layer, offs[j] * unit, 0))],
            out_specs=pl.BlockSpec((tm, tn), lambda i, j, offs: (i, j))),
        compiler_params=_cparams(("parallel", "arbitrary")),
        name="in_proj",
    )(offs, xn, w_in_t)


def _matmul(a, w3, layer, residual=None, tm_target=2304, tn_target=256, name="matmul"):
    m, k = a.shape
    n = w3.shape[2]
    tm = _pick(m, tm_target, 16)
    tn = _pick(n, tn_target, LANES)
    in_specs = [pl.BlockSpec((tm, k), lambda i, j: (i, 0)),
                pl.BlockSpec((None, k, tn), lambda i, j: (layer, 0, j))]
    args = [a, w3]
    body = _mm_kernel
    if residual is not None:
        in_specs.append(pl.BlockSpec((tm, tn), lambda i, j: (i, j)))
        args.append(residual)
        body = _mm_res_kernel
    return pl.pallas_call(
        body,
        out_shape=jax.ShapeDtypeStruct((m, n), F32),
        grid=(m // tm, n // tn),
        in_specs=in_specs,
        out_specs=pl.BlockSpec((tm, tn), lambda i, j: (i, j)),
        compiler_params=_cparams(("parallel", "arbitrary")),
        name=name,
    )(*args)


def _prep_kernel(cq_ref, ckv_ref, kpe_ref, c_ref, s1_ref, s2_ref, gcq_ref, gckv_ref, wuq_ref, wukt_ref,
                 gqn_ref, gqr_ref, gkn_ref, gkr_ref,
                 clat_ref, krot_ref, q_ref, kcat_ref, rs_ref):
    tm = cq_ref.shape[0]
    cs, s1, s2 = c_ref[...], s1_ref[...], s2_ref[...]

    def rope64(x):
        return x * cs + pltpu.roll(x, 96, 1) * s1 + pltpu.roll(x, 32, 1) * s2

    low = lax.broadcasted_iota(jnp.int32, (tm, LANES), 1) < MLA_ROPE

    c_lat = _rms(ckv_ref[...], gckv_ref[...])
    clat_ref[...] = c_lat
    k_rot = jnp.where(low, rope64(kpe_ref[...]), 0.0)
    krot_ref[...] = k_rot[:, :MLA_ROPE]
    clb = c_lat.astype(BF)
    kcat_ref[:, 0:KV_LORA] = clb
    kcat_ref[:, KV_LORA:QPAD] = k_rot.astype(BF)

    knt = lax.dot_general(wukt_ref[...], clb, _NT, preferred_element_type=F32)
    ssq = jnp.sum((knt * knt).reshape(MLA_HEADS, MLA_NOPE, tm), axis=1)
    rs_ref[...] = lax.rsqrt((ssq + _lane_sumsq_t(k_rot)) / MLA_QK + NORM_EPS)

    cqn = _rms(cq_ref[...], gcq_ref[...]).astype(BF)
    q = jnp.dot(cqn, wuq_ref[...], preferred_element_type=F32)
    gqn, gqr, gkn, gkr = gqn_ref[...], gqr_ref[...], gkn_ref[...], gkr_ref[...]
    nope_w = MLA_HEADS * MLA_NOPE
    for j in range(MLA_HEADS // 2):
        xr = rope64(q[:, nope_w + LANES * j: nope_w + LANES * (j + 1)])
        sq = xr * xr
        ss = (jnp.sum(jnp.where(low, sq, 0.0), axis=-1, keepdims=True),
              jnp.sum(jnp.where(low, 0.0, sq), axis=-1, keepdims=True))
        for e in range(2):
            h = 2 * j + e
            nope = q[:, MLA_NOPE * h: MLA_NOPE * (h + 1)]
            r = lax.rsqrt((jnp.sum(nope * nope, axis=-1, keepdims=True) + ss[e]) / MLA_QK + NORM_EPS)
            qg = (nope * r * gqn * gkn).astype(BF)
            q_ref[h, :, 0:KV_LORA] = jnp.dot(qg, wukt_ref[MLA_NOPE * h: MLA_NOPE * (h + 1), :],
                                             preferred_element_type=F32)
            rp = xr * r * gqr * gkr
            if e == 1:
                rp = pltpu.roll(rp, MLA_ROPE, 1)
            q_ref[h, :, KV_LORA:QPAD] = jnp.where(low, rp, 0.0)


def _mla_prep(z, offs, tabs, gcq, gckv, wuq_p, wukt, gqn, gqr, gkn, gkr):
    t = z.shape[0]
    tm = _pick(t, 256, 16)
    cs, s1, s2 = tabs

    def zs(width, off):
        return pl.BlockSpec((tm, width), lambda i: (i, off // width))

    row = lambda w: pl.BlockSpec((tm, w), lambda i: (i, 0))
    full = lambda a: pl.BlockSpec(a.shape, lambda i: (0,) * a.ndim)
    small = [gcq, gckv, wuq_p, wukt, gqn, gqr, gkn, gkr]
    return pl.pallas_call(
        _prep_kernel,
        out_shape=(jax.ShapeDtypeStruct((t, KV_LORA), F32),
                   jax.ShapeDtypeStruct((t, MLA_ROPE), F32),
                   jax.ShapeDtypeStruct((MLA_HEADS, t, QPAD), F32),
                   jax.ShapeDtypeStruct((t, QPAD), BF),
                   jax.ShapeDtypeStruct((MLA_HEADS, t), F32)),
        grid=(t // tm,),
        in_specs=[zs(Q_LORA, offs["cq"]), zs(KV_LORA, offs["ckv"]), zs(LANES, offs["kpe"]),
                  row(LANES), row(LANES), row(LANES)] + [full(a) for a in small],
        out_specs=(row(KV_LORA), row(MLA_ROPE),
                   pl.BlockSpec((MLA_HEADS, tm, QPAD), lambda i: (0, i, 0)),
                   row(QPAD),
                   pl.BlockSpec((MLA_HEADS, tm), lambda i: (0, i))),
        compiler_params=_cparams(("parallel",)),
        name="mla_prep",
    )(z, z, z, cs, s1, s2, *small)


PROMPT_HEAD_GROUPS = 4


def _attn_prompt_kernel(q_ref, k_ref, rs_ref, wuv_ref, o_ref, qb_sc, m_sc, l_sc, acc_sc, *, bq, bk):
    qi = pl.program_id(1)
    ki = pl.program_id(2)
    rows = MLA_HEADS * bq

    @pl.when(ki == 0)
    def _():
        qb_sc[...] = q_ref[...].reshape(rows, QPAD).astype(BF)
        m_sc[...] = jnp.full(m_sc.shape, NEG_INF, F32)
        l_sc[...] = jnp.zeros(l_sc.shape, F32)
        acc_sc[...] = jnp.zeros(acc_sc.shape, F32)

    def update(masked):
        k = k_ref[...]
        rs = rs_ref[...] * ATTN_SCALE
        hg = MLA_HEADS // PROMPT_HEAD_GROUPS
        gr = hg * bq
        for g in range(PROMPT_HEAD_GROUPS):
            rsl = slice(g * gr, (g + 1) * gr)
            s = lax.dot_general(qb_sc[rsl, :], k, _NT, preferred_element_type=F32)
            s3 = s.reshape(hg, bq, bk) * rs[g * hg:(g + 1) * hg, None, :]
            if masked:
                qpos = qi * bq + lax.broadcasted_iota(jnp.int32, (1, bq, bk), 1)
                kpos = ki * bk + lax.broadcasted_iota(jnp.int32, (1, bq, bk), 2)
                s3 = jnp.where(kpos <= qpos, s3, NEG_INF)
            s2 = s3.reshape(gr, bk)
            m_prev = m_sc[rsl, :]
            m_new = jnp.maximum(m_prev, jnp.max(s2, axis=-1, keepdims=True))
            alpha = jnp.exp(m_prev - m_new)
            p = jnp.exp(s2 - m_new)
            l_sc[rsl, :] = alpha * l_sc[rsl, :] + jnp.sum(p, axis=-1, keepdims=True)
            acc_sc[rsl, :] = alpha * acc_sc[rsl, :] + jnp.dot(p.astype(BF), k[:, 0:KV_LORA],
                                                              preferred_element_type=F32)
            m_sc[rsl, :] = m_new

    first_q, last_q = qi * bq, qi * bq + (bq - 1)
    last_k = ki * bk + (bk - 1)

    @pl.when(last_k <= first_q)
    def _():
        update(False)

    @pl.when((last_k > first_q) & (ki * bk <= last_q))
    def _():
        update(True)

    @pl.when(ki == pl.num_programs(2) - 1)
    def _():
        ol = (acc_sc[...] / l_sc[...]).astype(BF)
        for h in range(MLA_HEADS):
            o_ref[:, MLA_V * h: MLA_V * (h + 1)] = jnp.dot(
                ol[h * bq:(h + 1) * bq], wuv_ref[:, MLA_V * h: MLA_V * (h + 1)],
                preferred_element_type=F32).astype(o_ref.dtype)


def _attn_prompt(q, kcat, rs, wuv, nb, s):
    bq = _pick(s, 128, 16)
    bk = _pick(s, 1024, LANES)
    nq, nk = s // bq, s // bk
    rows = MLA_HEADS * bq

    def kblk(b, qi, ki):
        return b * nk + jnp.minimum(ki, (qi * bq + bq - 1) // bk)

    return pl.pallas_call(
        functools.partial(_attn_prompt_kernel, bq=bq, bk=bk),
        out_shape=jax.ShapeDtypeStruct((nb * s, MLA_HEADS * MLA_V), BF),
        grid=(nb, nq, nk),
        in_specs=[pl.BlockSpec((MLA_HEADS, bq, QPAD), lambda b, qi, ki: (0, b * nq + qi, 0)),
                  pl.BlockSpec((bk, QPAD), lambda b, qi, ki: (kblk(b, qi, ki), 0)),
                  pl.BlockSpec((MLA_HEADS, bk), lambda b, qi, ki: (0, kblk(b, qi, ki))),
                  pl.BlockSpec(wuv.shape, lambda b, qi, ki: (0, 0))],
        out_specs=pl.BlockSpec((bq, MLA_HEADS * MLA_V), lambda b, qi, ki: (b * nq + qi, 0)),
        scratch_shapes=[pltpu.VMEM((rows, QPAD), BF), pltpu.VMEM((rows, 1), F32),
                        pltpu.VMEM((rows, 1), F32), pltpu.VMEM((rows, KV_LORA), F32)],
        compiler_params=_cparams(("parallel", "parallel", "arbitrary")),
        name="attn_prompt",
    )(q, kcat, rs, wuv)


def _attn_paged_kernel(pt_ref, q_ref, cnew_ref, krnew_ref, wukt_ref, wuv_ref, ckv_hbm, kpe_hbm, o_ref,
                       wq_sc, qr_sc, m_sc, l_sc, acc_sc, cpad_sc, krpad_sc, sprev_sc, cbprev_sc,
                       cbuf, krbuf, sem, *, pp, dq, layer):
    j = pl.program_id(1)
    step = pl.program_id(0) * pl.num_programs(1) + j
    last_step = pl.num_programs(0) * pl.num_programs(1) - 1
    slot = step % 2
    rows = MLA_HEADS * dq
    nk = MLA_HEADS * MLA_NOPE

    def page_copies(src_step, dst_slot):
        cps = []
        for i in range(pp):
            page = pt_ref[src_step * pp + i]
            cps.append(pltpu.make_async_copy(ckv_hbm.at[layer, page], cbuf.at[dst_slot, i], sem.at[dst_slot]))
            cps.append(pltpu.make_async_copy(kpe_hbm.at[layer, page], krbuf.at[dst_slot, i], sem.at[dst_slot]))
        return cps

    def fetch_next_and_wait():
        for cp in page_copies(jnp.minimum(step + 1, last_step), 1 - slot):
            cp.start()
        for cp in page_copies(step, slot):
            cp.wait()

    @pl.when(step == 0)
    def _():
        for cp in page_copies(0, 0):
            cp.start()

    @pl.when(j == 0)
    def _():
        q = q_ref[...].reshape(rows, QPAD)
        wq_sc[0:nk, :] = wukt_ref[...]
        wq_sc[nk:nk + rows, :] = q[:, 0:KV_LORA].astype(BF)
        qr_sc[...] = q[:, KV_LORA:KV_LORA + MLA_ROPE].astype(BF)
        m_sc[...] = jnp.full(m_sc.shape, NEG_INF, F32)
        l_sc[...] = jnp.zeros(l_sc.shape, F32)
        acc_sc[...] = jnp.zeros(acc_sc.shape, F32)

    def scores(c, krt):
        n = c.shape[0]
        cb = c.astype(BF)
        big = lax.dot_general(wq_sc[...], cb, _NT, preferred_element_type=F32)
        knt = big[0:nk]
        ssq = jnp.sum((knt * knt).reshape(MLA_HEADS, MLA_NOPE, n), axis=1)
        krsq = jnp.sum(krt * krt, axis=0, keepdims=True)
        rs = lax.rsqrt((ssq + krsq) / MLA_QK + NORM_EPS) * ATTN_SCALE
        s = big[nk:nk + rows] + jnp.dot(qr_sc[...], krt.astype(BF), preferred_element_type=F32)
        return (s.reshape(MLA_HEADS, dq, n) * rs[:, None, :]).reshape(rows, n), cb

    def softmax_update(s, cb):
        m_prev = m_sc[...]
        m_new = jnp.maximum(m_prev, jnp.max(s, axis=-1, keepdims=True))
        alpha = jnp.exp(m_prev - m_new)
        p = jnp.exp(s - m_new)
        l_sc[...] = alpha * l_sc[...] + jnp.sum(p, axis=-1, keepdims=True)
        acc_sc[...] = alpha * acc_sc[...] + jnp.dot(p.astype(BF), cb, preferred_element_type=F32)
        m_sc[...] = m_new

    def page_chunk():
        return (jnp.concatenate([cbuf[slot, i] for i in range(pp)], axis=0),
                jnp.concatenate([krbuf[slot, i] for i in range(pp)], axis=1))

    @pl.when(j == 0)
    def _():
        fetch_next_and_wait()
        s, cb = scores(*page_chunk())
        sprev_sc[...] = s
        cbprev_sc[...] = cb
        cpad_sc[...] = jnp.zeros(cpad_sc.shape, F32)
        krpad_sc[...] = jnp.zeros(krpad_sc.shape, F32)
        cpad_sc[0:dq, :] = cnew_ref[...]
        krpad_sc[:, 0:dq] = krnew_ref[...]
        t = lax.broadcasted_iota(jnp.int32, (rows, PAGE_SIZE), 1)
        r = lax.broadcasted_iota(jnp.int32, (MLA_HEADS, dq, PAGE_SIZE), 1).reshape(rows, PAGE_SIZE)
        s_new, cb_new = scores(cpad_sc[...], krpad_sc[...])
        softmax_update(jnp.where(t <= r, s_new, NEG_INF), cb_new)

    @pl.when(j > 0)
    def _():
        fetch_next_and_wait()
        s_prev, cb_prev = sprev_sc[...], cbprev_sc[...]
        s, cb = scores(*page_chunk())
        softmax_update(s_prev, cb_prev)
        sprev_sc[...] = s
        cbprev_sc[...] = cb

    @pl.when(step == last_step)
    def _():
        for cp in page_copies(last_step, 1 - slot):
            cp.wait()

    @pl.when(j == pl.num_programs(1) - 1)
    def _():
        softmax_update(sprev_sc[...], cbprev_sc[...])
        ol = (acc_sc[...] / l_sc[...]).astype(BF)
        full = jnp.dot(ol, wuv_ref[...], preferred_element_type=F32)
        for h in range(MLA_HEADS):
            o_ref[:, MLA_V * h: MLA_V * (h + 1)] = full[h * dq:(h + 1) * dq, MLA_V * h: MLA_V * (h + 1)]


PAGES_PER_STEP = 16


def _attn_paged(page_table, q, c_new, kr_new_t, wukt, wuv, cache_ckv, cache_kpe_t, layer, row_blk0):
    nseq, n_pages = page_table.shape
    dq = c_new.shape[1]
    pp = math.gcd(n_pages, PAGES_PER_STEP)
    n_chunks = n_pages // pp
    pt = page_table.reshape(-1)

    in_specs = [pl.BlockSpec((MLA_HEADS, dq, QPAD), lambda b, j, pt_ref: (0, row_blk0 + b, 0)),
                pl.BlockSpec((None, dq, KV_LORA), lambda b, j, pt_ref: (b, 0, 0)),
                pl.BlockSpec((None, MLA_ROPE, dq), lambda b, j, pt_ref: (b, 0, 0)),
                pl.BlockSpec(wukt.shape, lambda b, j, pt_ref: (0, 0)),
                pl.BlockSpec(wuv.shape, lambda b, j, pt_ref: (0, 0)),
                pl.BlockSpec(memory_space=pl.ANY), pl.BlockSpec(memory_space=pl.ANY)]
    rows = MLA_HEADS * dq
    return pl.pallas_call(
        functools.partial(_attn_paged_kernel, pp=pp, dq=dq, layer=layer),
        out_shape=jax.ShapeDtypeStruct((nseq, dq, MLA_HEADS * MLA_V), F32),
        grid_spec=pltpu.PrefetchScalarGridSpec(
            num_scalar_prefetch=1,
            grid=(nseq, n_chunks),
            in_specs=in_specs,
            out_specs=pl.BlockSpec((None, dq, MLA_HEADS * MLA_V), lambda b, j, pt_ref: (b, 0, 0)),
            scratch_shapes=[pltpu.VMEM((MLA_HEADS * MLA_NOPE + rows, KV_LORA), BF),
                            pltpu.VMEM((rows, MLA_ROPE), BF),
                            pltpu.VMEM((rows, 1), F32), pltpu.VMEM((rows, 1), F32),
                            pltpu.VMEM((rows, KV_LORA), F32),
                            pltpu.VMEM((PAGE_SIZE, KV_LORA), F32), pltpu.VMEM((MLA_ROPE, PAGE_SIZE), F32),
                            pltpu.VMEM((rows, pp * PAGE_SIZE), F32), pltpu.VMEM((pp * PAGE_SIZE, KV_LORA), BF),
                            pltpu.VMEM((2, pp, PAGE_SIZE, KV_LORA), F32),
                            pltpu.VMEM((2, pp, MLA_ROPE, PAGE_SIZE), F32),
                            pltpu.SemaphoreType.DMA((2,))]),
        compiler_params=_cparams(("arbitrary", "arbitrary")),
        name="attn_paged",
    )(pt, q, c_new, kr_new_t, wukt, wuv, cache_ckv, cache_kpe_t)


def _ret_kernel(rq_ref, rk_ref, rv_ref, rg_ref, cf_ref, sf_ref, g_ref, *rest, chunk, sb, has_state):
    if has_state:
        st_ref, o_ref, so_ref, r_sc, kp_sc, kd_sc, vp_sc, o_sc = rest
    else:
        o_ref, so_ref, r_sc, kp_sc, kd_sc, vp_sc, o_sc = rest
    ci = pl.program_id(1)
    lp = kp_sc.shape[0]

    @pl.when(ci == 0)
    def _():
        if has_state:
            r_sc[...] = st_ref[...]
        else:
            r_sc[...] = jnp.zeros(r_sc.shape, F32)

    if chunk < lp:
        kp_sc[...] = jnp.zeros(kp_sc.shape, F32)
        kd_sc[...] = jnp.zeros(kd_sc.shape, F32)
        vp_sc[...] = jnp.zeros(vp_sc.shape, F32)

    def pad(x, sc):
        if chunk == lp:
            return x
        sc[0:chunk, :] = x
        return sc[...]

    t = lax.broadcasted_iota(jnp.int32, (chunk, 1), 0).astype(F32)
    rel = (lax.broadcasted_iota(jnp.int32, (chunk, lp), 0)
           - lax.broadcasted_iota(jnp.int32, (chunk, lp), 1)).astype(F32)
    for si in range(sb):
        rows = slice(si * chunk, (si + 1) * chunk)
        cf, sf = cf_ref[rows, :], sf_ref[rows, :]
        rope = lambda x: x * cf + pltpu.roll(x, RET_DK // 2, 1) * sf
        for h in range(RET_HEADS):
            lg = LOG_G[h]
            q = rope(rq_ref[rows, RET_DK * h: RET_DK * (h + 1)])
            k = rope(rk_ref[rows, RET_DK * h: RET_DK * (h + 1)]) * (RET_DK ** -0.5)
            v = rv_ref[rows, RET_DV * h: RET_DV * (h + 1)]
            qb = q.astype(BF)
            kp = pad(k, kp_sc).astype(BF)
            vp = pad(v, vp_sc).astype(BF)
            decay = jnp.where(rel >= 0.0, jnp.exp(lg * jnp.maximum(rel, 0.0)), 0.0)
            s = lax.dot_general(qb, kp, _NT, preferred_element_type=F32) * decay
            r_old = r_sc[si, h]
            o = jnp.dot(s.astype(BF), vp, preferred_element_type=F32)
            o = o + jnp.dot(qb, r_old.astype(BF), preferred_element_type=F32) * jnp.exp(lg * (t + 1.0))
            kd = pad(k * jnp.exp(lg * (chunk - 1.0 - t)), kd_sc)
            r_sc[si, h] = math.exp(lg * chunk) * r_old + jnp.dot(
                kd.T.astype(BF), vp, preferred_element_type=F32)
            on = _rms(o, g_ref[h:h + 1, :])
            rg = rg_ref[rows, RET_DV * h: RET_DV * (h + 1)]
            o_sc[rows, RET_DV * h: RET_DV * (h + 1)] = rg * jax.nn.sigmoid(rg) * on
    o_ref[...] = o_sc[...].astype(o_ref.dtype)

    @pl.when(ci == pl.num_programs(1) - 1)
    def _():
        so_ref[...] = r_sc[...]


def _retention(z, offs, tabs, ret_g, nseq, seq_len, row0, state, layer=0):
    chunk = RET_CHUNK if seq_len % RET_CHUNK == 0 else seq_len
    nc = seq_len // chunk
    sb = 1 if nc > 1 else _pick(nseq, max(1, 64 // chunk), 1)
    rows = sb * chunk
    lp = max(chunk, LANES)
    blk0 = row0 // rows
    cf, sf = tabs
    dk, dv = RET_HEADS * RET_DK, RET_HEADS * RET_DV

    def zs(width, off):
        return pl.BlockSpec((rows, width), lambda b, c: (blk0 + b * nc + c, off // width))

    tab = pl.BlockSpec((rows, RET_DK), lambda b, c: (blk0 + b * nc + c, 0))
    st_spec = pl.BlockSpec((sb, RET_HEADS, RET_DK, RET_DV), lambda b, c: (b, 0, 0, 0))
    in_specs = [zs(dk, offs["rq"]), zs(dk, offs["rk"]), zs(dv, offs["rv"]), zs(dv, offs["rg"]), tab, tab,
                pl.BlockSpec(ret_g.shape, lambda b, c: (0, 0))]
    args = [z, z, z, z, cf, sf, ret_g]
    if state is not None:
        in_specs.append(pl.BlockSpec((None, sb, RET_HEADS, RET_DK, RET_DV), lambda b, c: (layer, b, 0, 0, 0)))
        args.append(state)
    return pl.pallas_call(
        functools.partial(_ret_kernel, chunk=chunk, sb=sb, has_state=state is not None),
        out_shape=(jax.ShapeDtypeStruct((nseq * seq_len, dv), BF),
                   jax.ShapeDtypeStruct((nseq, RET_HEADS, RET_DK, RET_DV), F32)),
        grid=(nseq // sb, nc),
        in_specs=in_specs,
        out_specs=(pl.BlockSpec((rows, dv), lambda b, c: (b * nc + c, 0)), st_spec),
        scratch_shapes=[pltpu.VMEM((sb, RET_HEADS, RET_DK, RET_DV), F32),
                        pltpu.VMEM((lp, RET_DK), F32), pltpu.VMEM((lp, RET_DK), F32),
                        pltpu.VMEM((lp, RET_DV), F32), pltpu.VMEM((rows, dv), F32)],
        compiler_params=_cparams(("parallel", "arbitrary")),
        name="retention",
    )(*args)


def _conv_kernel(cb_ref, cc_ref, cu_ref, w_ref, *rest, sb, rows_per_seq, has_state):
    if has_state:
        buf_ref, o_ref, so_ref, carry_sc = rest
    else:
        o_ref, so_ref, carry_sc = rest
    ti = pl.program_id(1)
    ln = rows_per_seq
    cd = cb_ref.shape[1]

    @pl.when(ti == 0)
    def _():
        if has_state:
            carry_sc[:, 6:8, :] = buf_ref[...]
        else:
            carry_sc[...] = jnp.zeros(carry_sc.shape, F32)

    w2 = cc_ref[...] * cu_ref[...]
    w3 = w2.reshape(sb, ln, cd)
    r1 = pltpu.roll(w2, 1, 0).reshape(sb, ln, cd)
    r2 = pltpu.roll(w2, 2, 0).reshape(sb, ln, cd)
    prev = carry_sc[...]
    p1, p2 = prev[:, 7:8, :], prev[:, 6:7, :]
    tpos = lax.broadcasted_iota(jnp.int32, (sb, ln, cd), 1)
    x1 = jnp.where(tpos == 0, p1, r1)
    x2 = jnp.where(tpos == 0, p2, jnp.where(tpos == 1, p1, r2))
    cw = w_ref[...]
    y = cw[0:1, :] * x2 + cw[1:2, :] * x1 + cw[2:3, :] * w3
    o_ref[...] = (cb_ref[...] * y.reshape(sb * ln, cd)).astype(o_ref.dtype)
    tail = w3[:, ln - 2:ln, :]
    carry_sc[:, 6:8, :] = tail

    @pl.when(ti == pl.num_programs(1) - 1)
    def _():
        so_ref[...] = tail


def _short_conv(z, offs, conv_w, nseq, seq_len, row0, state):
    if seq_len > 512:
        sb, ln = 1, _pick(seq_len, 512)
    else:
        ln = seq_len
        sb = _pick(nseq, max(1, 256 // ln), 1)
    nt = seq_len // ln
    rows = sb * ln
    blk0 = row0 // rows

    def zs(off):
        return pl.BlockSpec((rows, CONV_D), lambda b, t: (blk0 + b * nt + t, off // CONV_D))

    st_spec = pl.BlockSpec((sb, CONV_W - 1, CONV_D), lambda b, t: (b, 0, 0))
    in_specs = [zs(offs["cb"]), zs(offs["cc"]), zs(offs["cu"]), pl.BlockSpec(conv_w.shape, lambda b, t: (0, 0))]
    args = [z, z, z, conv_w]
    if state is not None:
        in_specs.append(st_spec)
        args.append(state)
    return pl.pallas_call(
        functools.partial(_conv_kernel, sb=sb, rows_per_seq=ln, has_state=state is not None),
        out_shape=(jax.ShapeDtypeStruct((nseq * seq_len, CONV_D), BF),
                   jax.ShapeDtypeStruct((nseq, CONV_W - 1, CONV_D), F32)),
        grid=(nseq // sb, nt),
        in_specs=in_specs,
        out_specs=(pl.BlockSpec((rows, CONV_D), lambda b, t: (b * nt + t, 0)), st_spec),
        scratch_shapes=[pltpu.VMEM((sb, 8, CONV_D), F32)],
        compiler_params=_cparams(("parallel", "arbitrary")),
        name="short_conv",
    )(*args)


def _merge_kernel(oa_ref, ob_ref, oc_ref, wa_ref, wb_ref, wc_ref, ga_ref, gb_ref, gc_ref, o_ref):
    def br(o, w, g):
        return jax.nn.sigmoid(g[...]) * jnp.dot(o[...], w[...].astype(BF), preferred_element_type=F32)

    m = br(oa_ref, wa_ref, ga_ref) + br(ob_ref, wb_ref, gb_ref) + br(oc_ref, wc_ref, gc_ref)
    o_ref[...] = m.astype(o_ref.dtype)


def _merge(oa, ob, oc, wa, wb, wc, layer, z, offs, d):
    t = oa.shape[0]
    tm = _pick(t, 1024, 16)
    tn = _pick(d, 512, LANES)
    osp = lambda a: pl.BlockSpec((tm, a.shape[1]), lambda i, j: (i, 0))
    wsp = lambda w: pl.BlockSpec((None, w.shape[1], tn), lambda i, j: (layer, 0, j))
    gsp = lambda off: pl.BlockSpec((tm, tn), lambda i, j: (i, off // tn + j))
    return pl.pallas_call(
        _merge_kernel,
        out_shape=jax.ShapeDtypeStruct((t, d), BF),
        grid=(t // tm, d // tn),
        in_specs=[osp(oa), osp(ob), osp(oc), wsp(wa), wsp(wb), wsp(wc),
                  gsp(offs["ga"]), gsp(offs["gb"]), gsp(offs["gc"])],
        out_specs=pl.BlockSpec((tm, tn), lambda i, j: (i, j)),
        compiler_params=_cparams(("parallel", "arbitrary")),
        name="branch_merge",
    )(oa, ob, oc, wa, wb, wc, z, z, z)


def _norm_router_kernel(x_ref, g_ref, whi_ref, wlo_ref, h_ref, lg_ref):
    y = _rms(x_ref[...], g_ref[...])
    h_ref[...] = y
    yhi = y.astype(BF)
    ylo = (y - yhi.astype(F32)).astype(BF)
    whi = whi_ref[...]
    lg_ref[...] = (jnp.dot(yhi, whi, preferred_element_type=F32)
                   + (jnp.dot(ylo, whi, preferred_element_type=F32)
                      + jnp.dot(yhi, wlo_ref[...], preferred_element_type=F32)))


def _norm_router(x, g, whi, wlo):
    t, d = x.shape
    tm = _pick(t, 512)
    n = whi.shape[1]
    return pl.pallas_call(
        _norm_router_kernel,
        out_shape=(jax.ShapeDtypeStruct((t, d), F32), jax.ShapeDtypeStruct((t, n), F32)),
        grid=(t // tm,),
        in_specs=[pl.BlockSpec((tm, d), lambda i: (i, 0)), pl.BlockSpec((1, d), lambda i: (0, 0)),
                  pl.BlockSpec((d, n), lambda i: (0, 0)), pl.BlockSpec((d, n), lambda i: (0, 0))],
        out_specs=(pl.BlockSpec((tm, d), lambda i: (i, 0)), pl.BlockSpec((tm, n), lambda i: (i, 0))),
        compiler_params=_cparams(("parallel",)),
        name="norm_router",
    )(x, g.reshape(1, d), whi, wlo)


def _row_copy(src_hbm, dst, sem, src_row, dst_slot, dst_row):
    return pltpu.make_async_copy(src_hbm.at[pl.ds(src_row, 1), :],
                                 dst.at[dst_slot, pl.ds(dst_row, 1), :], sem.at[dst_slot])


def _gather_rows(idx_ref, src_hbm, dst, sem, slot, n):
    def body(r, carry):
        _row_copy(src_hbm, dst, sem, idx_ref[0, r], slot, r).start()
        return carry
    lax.fori_loop(0, n, body, 0, unroll=8)


def _wait_rows(src_hbm, dst, sem, slot, n):
    def body(r, carry):
        _row_copy(src_hbm, dst, sem, 0, slot, r).wait()
        return carry
    lax.fori_loop(0, n, body, 0, unroll=8)


def _expert_kernel(be_ref, nu_ref, chg_ref, ws_ref, nx_ref, tok_ref, tok_next_ref, h_hbm,
                   wg_hbm, wu_hbm, wd_hbm, y_ref,
                   xbuf, sem, wg_sc, wu_sc, wd_sc, wg_st, wu_st, wd_st, wsem, *, layer):
    b = pl.program_id(0)
    nu = nu_ref[0]
    slot = b % 2
    bm = xbuf.shape[1]

    def weight_copies(e, ws):
        return [pltpu.make_async_copy(wg_hbm.at[layer, e], wg_st.at[ws], wsem.at[ws]),
                pltpu.make_async_copy(wu_hbm.at[layer, e], wu_st.at[ws], wsem.at[ws]),
                pltpu.make_async_copy(wd_hbm.at[layer, e], wd_st.at[ws], wsem.at[ws])]

    @pl.when(b == 0)
    def _():
        _gather_rows(tok_ref, h_hbm, xbuf, sem, 0, bm)
        for cp in weight_copies(be_ref[0], 0):
            cp.start()

    @pl.when(b < nu)
    def _():
        _wait_rows(h_hbm, xbuf, sem, slot, bm)

        @pl.when(chg_ref[b] == 1)
        def _():
            ws = ws_ref[b]
            for cp in weight_copies(be_ref[b], ws):
                cp.wait()
            wg_sc[...] = wg_st[ws].astype(BF)
            wu_sc[...] = wu_st[ws].astype(BF)
            wd_sc[...] = wd_st[ws].astype(BF)

            @pl.when(nx_ref[b] >= 0)
            def _():
                for cp in weight_copies(nx_ref[b], 1 - ws):
                    cp.start()

        for r in range(bm):
            _row_copy(h_hbm, xbuf, sem, tok_next_ref[0, r], 1 - slot, r).start()

        x = xbuf[slot].astype(BF)
        a = jnp.dot(x, wg_sc[...], preferred_element_type=F32)
        u = jnp.dot(x, wu_sc[...], preferred_element_type=F32)
        hid = (a * jax.nn.sigmoid(a) * u).astype(BF)
        y_ref[...] = jnp.dot(hid, wd_sc[...], preferred_element_type=F32)

    @pl.when(b == nu - 1)
    def _():
        _wait_rows(h_hbm, xbuf, sem, 1 - slot, bm)

    @pl.when(b >= nu)
    def _():
        y_ref[...] = jnp.zeros(y_ref.shape, F32)


def _experts(hn, slot_tok, block_expert, n_used, wg, wu, wd, layer):
    n_blocks = block_expert.shape[0]
    bm = MOE_BM
    d = hn.shape[1]
    ff = wg.shape[3]
    tok3 = slot_tok.reshape(n_blocks, 1, bm)
    smem_blk = lambda f: pl.BlockSpec((None, 1, bm), f, memory_space=pltpu.SMEM)
    blk = jnp.arange(n_blocks, dtype=jnp.int32)
    prev = jnp.concatenate([jnp.full((1,), -1, jnp.int32), block_expert[:-1]])
    changed = (block_expert != prev) & (blk < n_used[0])
    wslot = ((jnp.cumsum(changed.astype(jnp.int32)) - 1) % 2).astype(jnp.int32)
    pos = jnp.where(changed, blk, n_blocks)
    nxt_pos = lax.cummin(jnp.concatenate([pos[1:], jnp.full((1,), n_blocks, jnp.int32)]), reverse=True)
    nxt = jnp.where(nxt_pos < n_blocks, block_expert[jnp.minimum(nxt_pos, n_blocks - 1)], -1).astype(jnp.int32)
    return pl.pallas_call(
        functools.partial(_expert_kernel, layer=layer),
        out_shape=jax.ShapeDtypeStruct((n_blocks * bm, d), F32),
        grid_spec=pltpu.PrefetchScalarGridSpec(
            num_scalar_prefetch=5,
            grid=(n_blocks,),
            in_specs=[smem_blk(lambda b, *_: (b, 0, 0)),
                      smem_blk(lambda b, *_: (jnp.minimum(b + 1, n_blocks - 1), 0, 0)),
                      pl.BlockSpec(memory_space=pl.ANY),
                      pl.BlockSpec(memory_space=pl.ANY), pl.BlockSpec(memory_space=pl.ANY),
                      pl.BlockSpec(memory_space=pl.ANY)],
            out_specs=pl.BlockSpec((bm, d), lambda b, *_: (b, 0)),
            scratch_shapes=[pltpu.VMEM((2, bm, d), F32), pltpu.SemaphoreType.DMA((2,)),
                            pltpu.VMEM((d, ff), BF), pltpu.VMEM((d, ff), BF), pltpu.VMEM((ff, d), BF),
                            pltpu.VMEM((2, d, ff), F32), pltpu.VMEM((2, d, ff), F32),
                            pltpu.VMEM((2, ff, d), F32), pltpu.SemaphoreType.DMA((2,))]),
        compiler_params=_cparams(("arbitrary",)),
        name="moe_experts",
    )(block_expert, n_used, changed.astype(jnp.int32), wslot, nxt, tok3, tok3, hn, wg, wu, wd)


def _combine_kernel(d_ref, d_next_ref, x_ref, g_ref, y_hbm, o_ref, ybuf, sem):
    i = pl.program_id(0)
    slot = i % 2
    n = ybuf.shape[1]
    tm = n // TOP_K

    @pl.when(i == 0)
    def _():
        _gather_rows(d_ref, y_hbm, ybuf, sem, 0, n)

    @pl.when(i + 1 < pl.num_programs(0))
    def _():
        _gather_rows(d_next_ref, y_hbm, ybuf, sem, 1 - slot, n)

    _wait_rows(y_hbm, ybuf, sem, slot, n)
    g = g_ref[...]
    o_ref[...] = x_ref[...] + (ybuf[slot, 0:tm, :] * g[:, 0:1] + ybuf[slot, tm:n, :] * g[:, 1:2])


def _combine(x, y, dest, gate):
    t, d = x.shape
    tm = _pick(t, 128)
    nt = t // tm
    d3 = dest.reshape(nt, tm, TOP_K).transpose(0, 2, 1).reshape(nt, 1, TOP_K * tm)
    smem_blk = lambda f: pl.BlockSpec((None, 1, TOP_K * tm), f, memory_space=pltpu.SMEM)
    return pl.pallas_call(
        _combine_kernel,
        out_shape=jax.ShapeDtypeStruct((t, d), F32),
        grid=(nt,),
        in_specs=[smem_blk(lambda i: (i, 0, 0)),
                  smem_blk(lambda i: (jnp.minimum(i + 1, nt - 1), 0, 0)),
                  pl.BlockSpec((tm, d), lambda i: (i, 0)),
                  pl.BlockSpec((tm, TOP_K), lambda i: (i, 0)),
                  pl.BlockSpec(memory_space=pl.ANY)],
        out_specs=pl.BlockSpec((tm, d), lambda i: (i, 0)),
        scratch_shapes=[pltpu.VMEM((2, TOP_K * tm, d), F32), pltpu.SemaphoreType.DMA((2,))],
        compiler_params=_cparams(("arbitrary",)),
        name="moe_combine",
    )(d3, d3, x, gate, y)


def _route(logits, bias_g, bias_e):
    t = logits.shape[0]
    g_logits = logits[:, :N_GROUPS] + bias_g
    e_logits = (logits[:, N_GROUPS:N_GROUPS + N_EXPERTS] + bias_e).reshape(t, N_GROUPS, EXPERTS_PER_GROUP)
    p_group = jax.nn.softmax(g_logits, axis=-1)
    grp = jnp.argmax(g_logits, axis=-1).astype(jnp.int32)
    gsel = grp[:, None] == jnp.arange(N_GROUPS, dtype=jnp.int32)[None, :]
    in_grp = jnp.sum(jnp.where(gsel[:, :, None], e_logits, 0.0), axis=1)
    eidx = jnp.arange(EXPERTS_PER_GROUP, dtype=jnp.int32)[None, :]
    i1 = jnp.argmax(in_grp, axis=-1).astype(jnp.int32)
    rest = jnp.where(eidx == i1[:, None], -jnp.inf, in_grp)
    i2 = jnp.argmax(rest, axis=-1).astype(jnp.int32)
    top_v = jnp.stack([jnp.max(in_grp, axis=-1), jnp.max(rest, axis=-1)], axis=-1)
    top_i = jnp.stack([i1, i2], axis=-1)
    gate = jax.nn.softmax(top_v, axis=-1) * jnp.max(p_group, axis=-1, keepdims=True)
    expert = grp[:, None] * EXPERTS_PER_GROUP + top_i

    bm = MOE_BM
    n_assign = t * TOP_K
    e_flat = expert.reshape(-1)
    onehot_b = e_flat[:, None] == jnp.arange(N_EXPERTS, dtype=jnp.int32)[None, :]
    onehot = onehot_b.astype(jnp.int32)
    rank_all = jnp.cumsum(onehot, axis=0)
    counts = rank_all[-1]
    padded = (counts + bm - 1) // bm * bm
    pad_end = jnp.cumsum(padded)
    pad_start = pad_end - padded
    dest = jnp.sum(jnp.where(onehot_b, rank_all - 1 + pad_start[None, :], 0), axis=1).astype(jnp.int32)
    n_blocks = (n_assign + N_EXPERTS * (bm - 1) + bm - 1) // bm
    n_slots = n_blocks * bm
    tok = jnp.repeat(jnp.arange(t, dtype=jnp.int32), TOP_K)
    slot_tok = (jnp.arange(n_slots, dtype=jnp.int32) % t).at[dest].set(tok)
    blk_start = jnp.arange(n_blocks, dtype=jnp.int32) * bm
    block_expert = jnp.minimum(
        jnp.sum((pad_end[None, :] <= blk_start[:, None]).astype(jnp.int32), axis=1),
        N_EXPERTS - 1).astype(jnp.int32)
    n_used = (pad_end[-1] // bm).astype(jnp.int32).reshape(1)
    return slot_tok, gate, block_expert, n_used, dest.reshape(t, TOP_K)


def _rope_tables(pos, half):
    inv_freq = ROPE_BASE ** (-jnp.arange(half, dtype=F32) / half)
    ang = pos.astype(F32)[:, None] * inv_freq[None, :]
    return jnp.cos(ang), jnp.sin(ang)


IN_PROJ_TN = 512


def _z_layout(d):
    segs = [("ga", d), ("gb", d), ("gc", d),
            ("rv", RET_HEADS * RET_DV), ("rg", RET_HEADS * RET_DV),
            ("cb", CONV_D), ("cc", CONV_D), ("cu", CONV_D),
            ("cq", Q_LORA), ("rq", RET_HEADS * RET_DK), ("rk", RET_HEADS * RET_DK),
            ("ckv", KV_LORA), ("kpe", IN_PROJ_TN - KV_LORA)]
    offs, o = {}, 0
    for name, w in segs:
        assert o % LANES == 0 and (name == "kpe" or o % w == 0)
        offs[name] = o
        o += w
    return segs, offs, o


def _in_proj_tiles(d):
    src_names = ["cq", "ckv", "kpe", "rq", "rk", "rv", "rg", "cb", "cc", "cu", "ga", "gb", "gc"]
    src_w = [Q_LORA, KV_LORA, MLA_ROPE, RET_HEADS * RET_DK, RET_HEADS * RET_DK, RET_HEADS * RET_DV,
             RET_HEADS * RET_DV, CONV_D, CONV_D, CONV_D, d, d, d]
    src_off = dict(zip(src_names, np.cumsum([0] + src_w[:-1]).tolist()))
    segs, offs, total = _z_layout(d)
    tn = IN_PROJ_TN
    tiles = []
    for name, w in segs[:-2]:
        assert w % tn == 0
        tiles += [src_off[name] + i * tn for i in range(w // tn)]
    assert [n for n, _ in segs[-2:]] == ["ckv", "kpe"] and src_off["kpe"] == src_off["ckv"] + KV_LORA
    assert KV_LORA + LANES <= tn and src_off["ckv"] + tn <= sum(src_w)
    tiles.append(src_off["ckv"])
    assert len(tiles) * tn == total
    return tiles, offs


def kernel(x_prompt, x_sample, cache_ckv, cache_kpe, state_ret, state_conv, page_table, norm_mix_g, w_in, mla_g_cq, mla_g_ckv, mla_w_uq, mla_w_uk, mla_w_uv, mla_g_qn, mla_g_kn, ret_g, conv_w, w_branch_a, w_branch_b, w_branch_c, w_out, norm_ffn_g, router_group_w, router_group_b, router_expert_w, router_expert_b, expert_w_gate, expert_w_up, expert_w_down):
    nb, s, d = x_prompt.shape
    ns, dq, _ = x_sample.shape
    depth = w_in.shape[0]
    tp, ts = nb * s, ns * dq
    past_len = page_table.shape[1] * PAGE_SIZE

    x = jnp.concatenate([x_prompt.reshape(tp, d), x_sample.reshape(ts, d)], axis=0)
    pos = jnp.concatenate([jnp.tile(jnp.arange(s, dtype=jnp.int32), nb),
                           jnp.tile(past_len + jnp.arange(dq, dtype=jnp.int32), ns)])
    cm, sm = _rope_tables(pos, MLA_ROPE // 2)
    zm = jnp.zeros_like(sm)
    mla_tabs = (jnp.concatenate([cm, cm, cm, cm], axis=1),
                jnp.concatenate([-sm, zm, -sm, zm], axis=1),
                jnp.concatenate([zm, sm, zm, sm], axis=1))
    cr, sr = _rope_tables(pos, RET_DK // 2)
    ret_tabs = (jnp.concatenate([cr, cr], axis=1), jnp.concatenate([-sr, sr], axis=1))

    uq_perm = np.concatenate(
        [h * MLA_QK + np.arange(MLA_NOPE) for h in range(MLA_HEADS)]
        + [h * MLA_QK + MLA_NOPE + np.arange(MLA_ROPE) for h in range(MLA_HEADS)])

    outs = {k: [] for k in ("ckv_p", "kpe_p", "ret_p", "conv_p", "ckv_s", "kpe_s", "ret_s", "conv_s")}
    w_in_t = jnp.swapaxes(w_in, 1, 2)
    cache_kpe_t = jnp.swapaxes(cache_kpe, 2, 3)
    in_tiles, offs = _in_proj_tiles(d)
    for l in range(depth):
        wuq_p = mla_w_uq[l][:, uq_perm].astype(BF)
        wukt = mla_w_uk[l].T.astype(BF)
        wuv = mla_w_uv[l].astype(BF)
        gqn = mla_g_qn[l][:MLA_NOPE].reshape(1, -1)
        gqr = jnp.tile(mla_g_qn[l][MLA_NOPE:], 2).reshape(1, -1)
        gkn = mla_g_kn[l][:MLA_NOPE].reshape(1, -1)
        gkr = jnp.tile(mla_g_kn[l][MLA_NOPE:], 2).reshape(1, -1)

        xn = _norm_cast(x, norm_mix_g[l])
        z = _in_proj(xn, w_in_t, l, in_tiles, IN_PROJ_TN)

        c_lat, k_rot, q, kcat, rs = _mla_prep(
            z, offs, mla_tabs, mla_g_cq[l].reshape(1, -1), mla_g_ckv[l].reshape(1, -1),
            wuq_p, wukt, gqn, gqr, gkn, gkr)
        oa_p = _attn_prompt(q, kcat, rs, wuv, nb, s)
        oa_s = _attn_paged(page_table, q, c_lat[tp:].reshape(ns, dq, KV_LORA),
                           jnp.swapaxes(k_rot[tp:].reshape(ns, dq, MLA_ROPE), 1, 2), wukt, wuv,
                           cache_ckv, cache_kpe_t, l, tp // dq)
        oa = jnp.concatenate([oa_p, oa_s.reshape(ts, -1).astype(BF)], axis=0)

        ob_p, ret_p = _retention(z, offs, ret_tabs, ret_g[l], nb, s, 0, None)
        ob_s, ret_s = _retention(z, offs, ret_tabs, ret_g[l], ns, dq, tp, state_ret, l)
        ob = jnp.concatenate([ob_p, ob_s], axis=0)

        oc_p, conv_p = _short_conv(z, offs, conv_w[l], nb, s, 0, None)
        oc_s, conv_s = _short_conv(z, offs, conv_w[l], ns, dq, tp, state_conv[l])
        oc = jnp.concatenate([oc_p, oc_s], axis=0)

        m = _merge(oa, ob, oc, w_branch_a, w_branch_b, w_branch_c, l, z, offs, d)
        x = _matmul(m, w_out, l, residual=x, tm_target=1024, tn_target=512, name="out_proj")

        wr = jnp.concatenate([router_group_w[l], router_expert_w[l]], axis=1)
        wr = jnp.pad(wr, ((0, 0), (0, LANES - wr.shape[1])))
        wr_hi = wr.astype(BF)
        wr_lo = (wr - wr_hi.astype(F32)).astype(BF)
        hn, logits = _norm_router(x, norm_ffn_g[l], wr_hi, wr_lo)
        slot_tok, gate, block_expert, n_used, dest = _route(logits, router_group_b[l], router_expert_b[l])
        y = _experts(hn, slot_tok, block_expert, n_used,
                     expert_w_gate, expert_w_up, expert_w_down, l)
        x = _combine(x, y, dest, gate)

        outs["ckv_p"].append(c_lat[:tp].reshape(nb, s, KV_LORA))
        outs["kpe_p"].append(k_rot[:tp].reshape(nb, s, MLA_ROPE))
        outs["ret_p"].append(ret_p)
        outs["conv_p"].append(conv_p)
        outs["ckv_s"].append(c_lat[tp:].reshape(ns, dq, KV_LORA))
        outs["kpe_s"].append(k_rot[tp:].reshape(ns, dq, MLA_ROPE))
        outs["ret_s"].append(ret_s)
        outs["conv_s"].append(conv_s)

    st = lambda k: jnp.stack(outs[k])
    return (x[:tp].reshape(nb, s, d), x[tp:].reshape(ns, dq, d),
            st("ckv_p"), st("kpe_p"), st("ret_p"), st("conv_p"),
            st("ckv_s"), st("kpe_s"), st("ret_s"), st("conv_s"))
```
